```python
import jax, jax.numpy as jnp
from jax import lax
import numpy as np

D_MODEL = 1024
BATCH = 2
SEQ = 16384
DEPTH = 1

HEAD_DIM = 64
ATTN_SCALE = HEAD_DIM ** -0.5
DIL_PAIRS = ((128, 1), (512, 4), (2048, 16))
N_DIL_GROUPS = len(DIL_PAIRS)
DIL_HEADS = 8
DIL_QBLK = 128
MOBA_HEADS = 8
MOBA_BLOCK = 256
MOBA_TOPK = 3
MOBA_QCHUNK = 64
MEM_TOKENS = 256
MEM_HEADS = 4
N_BRANCHES = 3
EPS = 1e-6
NEG_INF = -1e30

D_A = DIL_HEADS * HEAD_DIM
D_B = MOBA_HEADS * HEAD_DIM
D_M = MEM_HEADS * HEAD_DIM
IN_SIZES = [D_A] * (3 * N_DIL_GROUPS) + [D_A] + [D_B] * 4 + [D_M] * 2 + [N_BRANCHES * D_MODEL]
D_IN = int(sum(IN_SIZES))
IN_SPLIT_POINTS = tuple(int(v) for v in np.cumsum(IN_SIZES)[:-1])

kernel_name = "hybrid_dilated_moba_memory_gated_block"


def rmsnorm(x, w):
    xf = x.astype(jnp.float32)
    y = xf * lax.rsqrt(jnp.mean(xf * xf, axis=-1, keepdims=True) + EPS)
    return (y * w.astype(jnp.float32)).astype(x.dtype)


def alibi_slopes(n):
    return jnp.asarray(2.0 ** (-8.0 * np.arange(1, n + 1) / n), dtype=jnp.float32)


def dilated_window_group(q, k, v, slopes, window, dilation):
    B, S, H, Dh = q.shape
    n = S // dilation
    nb = n // DIL_QBLK
    span = window // dilation

    def to_blocks(t):
        t = t.reshape(B, n, dilation, H, Dh).transpose(0, 2, 3, 1, 4)
        return t.reshape(B, dilation, H, nb, DIL_QBLK, Dh)

    def band(t):
        prev = jnp.pad(t[:, :, :, :-1], ((0, 0), (0, 0), (0, 0), (1, 0), (0, 0), (0, 0)))
        return jnp.concatenate([prev, t], axis=4)

    qb = to_blocks(q)
    kb = band(to_blocks(k))
    vb = band(to_blocks(v))
    s = jnp.einsum("brhnqd,brhnkd->brhnqk", qb, kb, preferred_element_type=jnp.float32) * ATTN_SCALE
    a = jnp.arange(DIL_QBLK)[:, None]
    c = jnp.arange(2 * DIL_QBLK)[None, :]
    j = a + DIL_QBLK - c
    blk = jnp.arange(nb)[:, None, None]
    valid = (j >= 0) & (j <= span) & ((blk > 0) | (c >= DIL_QBLK))
    bias = -slopes[:, None, None] * (j * dilation).astype(jnp.float32)
    s = jnp.where(valid[None, None, None], s + bias[None, None, :, None], NEG_INF)
    m = jnp.max(s, axis=-1, keepdims=True)
    e = jnp.exp(s - m)
    den = jnp.sum(e, axis=-1)
    o = jnp.einsum("brhnqk,brhnkd->brhnqd", e.astype(v.dtype), vb,
                   preferred_element_type=jnp.float32) / den[..., None]
    lse = m[..., 0] + jnp.log(den)
    o = o.reshape(B, dilation, H, n, Dh).transpose(0, 3, 1, 2, 4).reshape(B, S, H, Dh)
    lse = lse.reshape(B, dilation, H, n).transpose(0, 3, 1, 2).reshape(B, S, H)
    return o, lse


def dilated_mixture(groups, slopes):
    B, S, H, Dh = groups[0][0].shape
    align = max(d for _, d in DIL_PAIRS) * DIL_QBLK
    S_pad = -(-S // align) * align

    def pad(t):
        return jnp.pad(t, ((0, 0), (0, S_pad - S), (0, 0), (0, 0)))

    outs, lses = [], []
    for (window, dilation), (q, k, v) in zip(DIL_PAIRS, groups):
        o, l = dilated_window_group(pad(q), pad(k), pad(v), slopes, window, dilation)
        outs.append(o[:, :S])
        lses.append(l[:, :S])
    w = jax.nn.softmax(jnp.stack(lses, 0), axis=0)
    o = jnp.einsum("gbsh,gbshd->bshd", w, jnp.stack(outs, 0))
    return o.astype(groups[0][0].dtype)


def moba_attention(q, k, v, slopes):
    B, S, H, Dh = q.shape
    S_pad = -(-S // MOBA_BLOCK) * MOBA_BLOCK
    padw = ((0, 0), (0, S_pad - S), (0, 0), (0, 0))
    q = jnp.pad(q, padw).transpose(0, 2, 1, 3)
    k = jnp.pad(k, padw).transpose(0, 2, 1, 3)
    v = jnp.pad(v, padw).transpose(0, 2, 1, 3)
    nblk = S_pad // MOBA_BLOCK
    kblk = k.reshape(B, H, nblk, MOBA_BLOCK, Dh)
    vblk = v.reshape(B, H, nblk, MOBA_BLOCK, Dh)
    kmean = jnp.mean(kblk.astype(jnp.float32), axis=3)
    gate = jnp.einsum("bhsd,bhnd->bhsn", q.astype(jnp.float32), kmean)
    qblk_id = jnp.arange(S_pad) // MOBA_BLOCK
    past = jnp.arange(nblk)[None, :] < qblk_id[:, None]
    gate = jnp.where(past[None, None], gate, NEG_INF)
    topk = min(MOBA_TOPK, nblk)
    _, sel = lax.top_k(gate, topk)
    sel_valid = jnp.arange(topk)[None, :] < qblk_id[:, None]
    bi = jnp.arange(B)[:, None, None, None]
    hi = jnp.arange(H)[None, :, None, None]
    koff = jnp.arange(MOBA_BLOCK)

    def chunk(ci):
        start = ci * MOBA_QCHUNK
        qc = lax.dynamic_slice_in_dim(q, start, MOBA_QCHUNK, axis=2)
        selc = lax.dynamic_slice_in_dim(sel, start, MOBA_QCHUNK, axis=2)
        validc = lax.dynamic_slice_in_dim(sel_valid, start, MOBA_QCHUNK, axis=0)
        k_sel = kblk[bi, hi, selc]
        v_sel = vblk[bi, hi, selc]
        own = start // MOBA_BLOCK
        k_own = lax.dynamic_index_in_dim(kblk, own, axis=2, keepdims=False)
        v_own = lax.dynamic_index_in_dim(vblk, own, axis=2, keepdims=False)
        t = start + jnp.arange(MOBA_QCHUNK)
        s_sel = jnp.einsum("bhqd,bhqnkd->bhqnk", qc, k_sel,
                           preferred_element_type=jnp.float32) * ATTN_SCALE
        pos_sel = selc[..., None] * MOBA_BLOCK + koff
        dist_sel = (t[None, None, :, None, None] - pos_sel).astype(jnp.float32)
        s_sel = s_sel - slopes[None, :, None, None, None] * dist_sel
        s_sel = jnp.where(validc[None, None, :, :, None], s_sel, NEG_INF)
        s_own = jnp.einsum("bhqd,bhkd->bhqk", qc, k_own,
                           preferred_element_type=jnp.float32) * ATTN_SCALE
        dist_own = t[:, None] - (own * MOBA_BLOCK + koff)[None, :]
        s_own = jnp.where(dist_own[None, None] >= 0,
                          s_own - slopes[None, :, None, None] * dist_own.astype(jnp.float32)[None, None],
                          NEG_INF)
        s_all = jnp.concatenate([s_sel.reshape(B, H, MOBA_QCHUNK, topk * MOBA_BLOCK), s_own], axis=-1)
        p = jax.nn.softmax(s_all, axis=-1).astype(v.dtype)
        p_sel = p[..., : topk * MOBA_BLOCK].reshape(B, H, MOBA_QCHUNK, topk, MOBA_BLOCK)
        p_own = p[..., topk * MOBA_BLOCK:]
        o = jnp.einsum("bhqnk,bhqnkd->bhqd", p_sel, v_sel, preferred_element_type=jnp.float32)
        o = o + jnp.einsum("bhqk,bhkd->bhqd", p_own, v_own, preferred_element_type=jnp.float32)
        return o

    n_chunks = S_pad // MOBA_QCHUNK
    o = lax.map(chunk, jnp.arange(n_chunks))
    o = o.transpose(1, 2, 0, 3, 4).reshape(B, H, S_pad, Dh).transpose(0, 2, 1, 3)[:, :S]
    return o.astype(q.dtype)


def memory_cross_attention(q, k, v):
    s = jnp.einsum("bshd,bmhd->bhsm", q, k, preferred_element_type=jnp.float32) * ATTN_SCALE
    p = jax.nn.softmax(s, axis=-1).astype(v.dtype)
    return jnp.einsum("bhsm,bmhd->bshd", p, v)


def setup_inputs(seed: int = 0) -> dict:
    key = jax.random.key(seed)
    ks = jax.random.split(key, 12)
    f32 = jnp.float32
    x = jax.random.normal(ks[0], (BATCH, SEQ, D_MODEL), f32)
    mem = jax.random.normal(ks[1], (BATCH, MEM_TOKENS, D_MODEL), f32)
    norm_w = 1.0 + 0.02 * jax.random.normal(ks[2], (DEPTH, D_MODEL), f32)
    mem_norm_w = 1.0 + 0.02 * jax.random.normal(ks[3], (DEPTH, D_MODEL), f32)
    w_in = jax.random.normal(ks[4], (DEPTH, D_MODEL, D_IN), f32) * D_MODEL ** -0.5
    b_merge = 0.1 * jax.random.normal(ks[5], (DEPTH, N_BRANCHES * D_MODEL), f32)
    w_mem_kv = jax.random.normal(ks[6], (DEPTH, D_MODEL, 2 * D_M), f32) * D_MODEL ** -0.5
    w_branch_a = jax.random.normal(ks[7], (DEPTH, D_A, D_MODEL), f32) * D_A ** -0.5
    w_branch_b = jax.random.normal(ks[8], (DEPTH, D_B, D_MODEL), f32) * D_B ** -0.5
    w_branch_m = jax.random.normal(ks[9], (DEPTH, D_M, D_MODEL), f32) * D_M ** -0.5
    w_out = jax.random.normal(ks[10], (DEPTH, D_MODEL, D_MODEL), f32) * D_MODEL ** -0.5
    final_norm_w = 1.0 + 0.02 * jax.random.normal(ks[11], (D_MODEL,), f32)
    return {"x": x, "mem": mem, "norm_w": norm_w, "mem_norm_w": mem_norm_w, "w_in": w_in,
            "b_merge": b_merge, "w_mem_kv": w_mem_kv, "w_branch_a": w_branch_a,
            "w_branch_b": w_branch_b, "w_branch_m": w_branch_m, "w_out": w_out,
            "final_norm_w": final_norm_w}


def reference(x, mem, norm_w, mem_norm_w, w_in, b_merge, w_mem_kv, w_branch_a, w_branch_b,
              w_branch_m, w_out, final_norm_w):
    B, S, _ = x.shape
    slopes = alibi_slopes(DIL_HEADS + MOBA_HEADS)
    slopes_a = slopes[0::2]
    slopes_b = slopes[1::2]

    def heads(t):
        return t.reshape(t.shape[0], t.shape[1], -1, HEAD_DIM)

    for l in range(DEPTH):
        h = rmsnorm(x, norm_w[l])
        proj = h @ w_in[l]
        pieces = jnp.split(proj, IN_SPLIT_POINTS, axis=-1)
        a_qkv = pieces[: 3 * N_DIL_GROUPS]
        a_gate = pieces[3 * N_DIL_GROUPS]
        b_q, b_k, b_v, b_gate = pieces[3 * N_DIL_GROUPS + 1: 3 * N_DIL_GROUPS + 5]
        m_q, m_gate = pieces[3 * N_DIL_GROUPS + 5: 3 * N_DIL_GROUPS + 7]
        merge_logits = pieces[3 * N_DIL_GROUPS + 7]

        groups = [(heads(a_qkv[3 * g]), heads(a_qkv[3 * g + 1]), heads(a_qkv[3 * g + 2]))
                  for g in range(N_DIL_GROUPS)]
        o_a = dilated_mixture(groups, slopes_a).reshape(B, S, D_A)
        o_b = moba_attention(heads(b_q), heads(b_k), heads(b_v), slopes_b).reshape(B, S, D_B)
        kv = rmsnorm(mem, mem_norm_w[l]) @ w_mem_kv[l]
        m_k, m_v = jnp.split(kv, 2, axis=-1)
        o_m = memory_cross_attention(heads(m_q), heads(m_k), heads(m_v)).reshape(B, S, D_M)

        p_a = (o_a * jax.nn.silu(a_gate)) @ w_branch_a[l]
        p_b = (o_b * jax.nn.silu(b_gate)) @ w_branch_b[l]
        p_m = (o_m * jax.nn.silu(m_gate)) @ w_branch_m[l]
        g_a, g_b, g_m = jnp.split(jax.nn.sigmoid(merge_logits + b_merge[l]), N_BRANCHES, axis=-1)
        merged = g_a * p_a + g_b * p_b + g_m * p_m
        x = x + merged @ w_out[l]
    return rmsnorm(x, final_norm_w)
```

```python
import functools

import jax
import jax.numpy as jnp
from jax import lax
from jax.experimental import pallas as pl
from jax.experimental.pallas import tpu as pltpu

F32 = jnp.float32
BF16 = jnp.bfloat16

D_MODEL = 1024
HEAD_DIM = 64
ATTN_SCALE = HEAD_DIM ** -0.5
DIL_PAIRS = ((128, 1), (512, 4), (2048, 16))
N_GROUPS = len(DIL_PAIRS)
DIL_HEADS = 8
DIL_QBLK = 128
MOBA_HEADS = 8
MOBA_BLOCK = 256
MOBA_TOPK = 3
MEM_HEADS = 4
N_BRANCHES = 3
EPS = 1e-6
NEG_INF = -1e30

D_A = DIL_HEADS * HEAD_DIM
D_B = MOBA_HEADS * HEAD_DIM
D_M = MEM_HEADS * HEAD_DIM
D_MERGE = N_BRANCHES * D_MODEL
D_IN = 3 * N_GROUPS * D_A + D_A + 4 * D_B + 2 * D_M + D_MERGE

COL_MERGE = 0
COL_A = D_MERGE
COL_A_GATE = COL_A + 3 * N_GROUPS * D_A
COL_BQ = COL_A_GATE + D_A
COL_BK = COL_BQ + D_B
COL_BV = COL_BK + D_B
COL_B_GATE = COL_BV + D_B
COL_MQ = COL_B_GATE + D_B
COL_M_GATE = COL_MQ + D_M

LANES = 128
HEADS_PER_LANE_TILE = LANES // HEAD_DIM

SLOPES_A = tuple(2.0 ** (-8.0 * k / 16) for k in range(1, 17, 2))
SLOPES_B = tuple(2.0 ** (-8.0 * k / 16) for k in range(2, 17, 2))

VMEM_LIMIT = 48 * 1024 * 1024


def _nt_dot(a, b):
    return lax.dot_general(a, b, (((1,), (1,)), ((), ())), preferred_element_type=F32)


def _split3(v):
    hi = v.astype(BF16)
    r1 = v - hi.astype(F32)
    mid = r1.astype(BF16)
    lo = (r1 - mid.astype(F32)).astype(BF16)
    return hi, mid, lo


def _inproj_kernel(x_ref, nw_ref, w_ref, o_ref, h_ref):
    @pl.when(pl.program_id(1) == 0)
    def _():
        xf = x_ref[...]
        ms = jnp.mean(xf * xf, axis=-1, keepdims=True)
        h_ref[...] = (xf * lax.rsqrt(ms + EPS) * nw_ref[...]).astype(BF16)

    o_ref[...] = jnp.dot(h_ref[...], w_ref[...], preferred_element_type=F32).astype(BF16)


def _inproj(x2, norm_w, w_bf16, tm=1024, tn=1536):
    T = x2.shape[0]
    return pl.pallas_call(
        _inproj_kernel,
        grid=(T // tm, D_IN // tn),
        in_specs=[
            pl.BlockSpec((tm, D_MODEL), lambda i, j: (i, 0)),
            pl.BlockSpec((1, D_MODEL), lambda i, j: (0, 0)),
            pl.BlockSpec((D_MODEL, tn), lambda i, j: (0, j)),
        ],
        out_specs=pl.BlockSpec((tm, tn), lambda i, j: (i, j)),
        out_shape=jax.ShapeDtypeStruct((T, D_IN), BF16),
        scratch_shapes=[pltpu.VMEM((tm, D_MODEL), BF16)],
        compiler_params=pltpu.CompilerParams(
            dimension_semantics=("parallel", "arbitrary"), vmem_limit_bytes=VMEM_LIMIT),
        name="inproj",
    )(x2, norm_w, w_bf16)


def _dilated_kernel(q_ref, kp_ref, kc_ref, vp_ref, vc_ref, o_ref, lse_ref, *, dilation, span):
    blk = pl.program_id(2)
    Q = DIL_QBLK
    row = lax.broadcasted_iota(jnp.int32, (Q, 2 * Q), 0)
    col = lax.broadcasted_iota(jnp.int32, (Q, 2 * Q), 1)
    steps = row + Q - col
    valid = (steps >= 0) & (steps <= span) & ((blk > 0) | (col >= Q))
    dist = (steps * dilation).astype(F32)
    lane = lax.broadcasted_iota(jnp.int32, (Q, LANES), 1)
    lse_tile = jnp.zeros((Q, LANES), F32)
    for h in range(DIL_HEADS):
        hs = slice(h * HEAD_DIM, (h + 1) * HEAD_DIM)
        q = q_ref[0, :, hs]
        k = jnp.concatenate([kp_ref[0, :, hs], kc_ref[0, :, hs]], axis=0)
        v = jnp.concatenate([vp_ref[0, :, hs], vc_ref[0, :, hs]], axis=0)
        s = _nt_dot(q, k) * ATTN_SCALE
        s = jnp.where(valid, s - SLOPES_A[h] * dist, NEG_INF)
        m = jnp.max(s, axis=-1, keepdims=True)
        e = jnp.exp(s - m)
        den = jnp.sum(e, axis=-1, keepdims=True)
        o = jnp.dot(e.astype(BF16), v, preferred_element_type=F32) / den
        o_ref[0, :, hs] = o
        lse_tile = jnp.where(lane == h, m + jnp.log(den), lse_tile)
    lse_ref[0] = lse_tile


def _dilated_group(proj, g, B, S):
    window, dilation = DIL_PAIRS[g]
    n = S // dilation
    nb = n // DIL_QBLK
    span = window // dilation
    pv = proj.reshape(B, n, dilation * D_IN)
    cpb = D_IN // D_A
    cq = COL_A // D_A + 3 * g

    def spec(piece, prev):
        if prev:
            return pl.BlockSpec((1, DIL_QBLK, D_A),
                                lambda b, r, i: (b, jnp.maximum(i - 1, 0), r * cpb + cq + piece))
        return pl.BlockSpec((1, DIL_QBLK, D_A), lambda b, r, i: (b, i, r * cpb + cq + piece))

    o, lse = pl.pallas_call(
        functools.partial(_dilated_kernel, dilation=dilation, span=span),
        grid=(B, dilation, nb),
        in_specs=[spec(0, False), spec(1, True), spec(1, False), spec(2, True), spec(2, False)],
        out_specs=[
            pl.BlockSpec((1, DIL_QBLK, D_A), lambda b, r, i: (b, i, r)),
            pl.BlockSpec((1, DIL_QBLK, LANES), lambda b, r, i: (b, i, r)),
        ],
        out_shape=[
            jax.ShapeDtypeStruct((B, n, dilation * D_A), F32),
            jax.ShapeDtypeStruct((B, n, dilation * LANES), F32),
        ],
        compiler_params=pltpu.CompilerParams(
            dimension_semantics=("parallel", "parallel", "arbitrary"), vmem_limit_bytes=VMEM_LIMIT),
        name=f"dilated_g{g}",
    )(pv, pv, pv, pv, pv)
    return o.reshape(B, S, D_A), lse.reshape(B, S, LANES)


def _moba_kernel(q_ref, k_ref, v_ref, o_ref, kmh_ref, kmm_ref, kml_ref, *, nblk):
    hp = pl.program_id(1)
    n = pl.program_id(2)
    BLK = MOBA_BLOCK

    @pl.when(n == 0)
    def _():
        def body(j, c):
            kb = k_ref[0, pl.ds(pl.multiple_of(j * BLK, BLK), BLK), :].astype(F32)
            km = jnp.sum(kb, axis=0, keepdims=True) * (1.0 / BLK)
            hi, mid, lo = _split3(km)
            kmh_ref[pl.ds(j, 1), :] = hi.astype(F32)
            kmm_ref[pl.ds(j, 1), :] = mid.astype(F32)
            kml_ref[pl.ds(j, 1), :] = lo.astype(F32)
            return c
        lax.fori_loop(0, nblk, body, 0)

    lane = lax.broadcasted_iota(jnp.int32, (BLK, LANES), 1)
    trow = lax.broadcasted_iota(jnp.int32, (BLK, BLK), 0)
    kcol = lax.broadcasted_iota(jnp.int32, (BLK, BLK), 1)
    rel = (trow - kcol).astype(F32)
    bcol = lax.broadcasted_iota(jnp.int32, (BLK, nblk), 1)
    bcol_f = bcol.astype(F32)

    q_all = q_ref[0] * ATTN_SCALE
    kmh = kmh_ref[...].astype(BF16)
    kmm = kmm_ref[...].astype(BF16)
    kml = kml_ref[...].astype(BF16)
    k_own = k_ref[0, pl.ds(pl.multiple_of(n * BLK, BLK), BLK), :]
    v_own = v_ref[0, pl.ds(pl.multiple_of(n * BLK, BLK), BLK), :]

    heads = []
    for hh in range(HEADS_PER_LANE_TILE):
        in_head = (lane >= hh * HEAD_DIM) & (lane < (hh + 1) * HEAD_DIM)
        qm = jnp.where(in_head, q_all, jnp.zeros_like(q_all))
        slope = jnp.float32(SLOPES_B[hh])
        for p in range(1, MOBA_HEADS // HEADS_PER_LANE_TILE):
            slope = jnp.where(hp == p, jnp.float32(SLOPES_B[HEADS_PER_LANE_TILE * p + hh]), slope)
        bias = -slope * rel
        gate = _nt_dot(qm, kmh) + _nt_dot(qm, kmm) + _nt_dot(qm, kml)
        g = jnp.where(bcol < n, gate, NEG_INF)
        sel = []
        for t in range(MOBA_TOPK):
            mx = jnp.max(g, axis=-1, keepdims=True)
            idx = jnp.min(jnp.where(g == mx, bcol_f, float(nblk)), axis=-1, keepdims=True)
            sel.append(jnp.where(t < n, idx, -1.0))
            g = jnp.where(bcol_f == idx, -jnp.inf, g)
        s = _nt_dot(qm, k_own) + jnp.where(kcol <= trow, bias, NEG_INF)
        m = jnp.max(s, axis=-1, keepdims=True)
        p = jnp.exp(s - m)
        l = jnp.sum(p, axis=-1, keepdims=True)
        acc = jnp.dot(p.astype(BF16), v_own, preferred_element_type=F32)
        heads.append((qm, slope, bias, sel, m, l, acc))

    def body(j, carry):
        kj = k_ref[0, pl.ds(pl.multiple_of(j * BLK, BLK), BLK), :]
        vj = v_ref[0, pl.ds(pl.multiple_of(j * BLK, BLK), BLK), :]
        jf = j.astype(F32)
        out = []
        for hh in range(HEADS_PER_LANE_TILE):
            qm, slope, bias, sel, _, _, _ = heads[hh]
            m, l, acc = carry[hh]
            chosen = (sel[0] == jf) | (sel[1] == jf) | (sel[2] == jf)
            shift = -slope * ((n - j) * BLK).astype(F32)
            rowterm = jnp.where(chosen, shift, NEG_INF)
            s = _nt_dot(qm, kj) + bias + rowterm
            m_new = jnp.maximum(m, jnp.max(s, axis=-1, keepdims=True))
            alpha = jnp.exp(m - m_new)
            p = jnp.exp(s - m_new)
            l = alpha * l + jnp.sum(p, axis=-1, keepdims=True)
            acc = alpha * acc + jnp.dot(p.astype(BF16), vj, preferred_element_type=F32)
            out.append((m_new, l, acc))
        return tuple(out)

    res = lax.fori_loop(0, n, body, tuple((h[4], h[5], h[6]) for h in heads))
    o0 = res[0][2] / res[0][1]
    o1 = res[1][2] / res[1][1]
    o_ref[0] = jnp.where(lane < HEAD_DIM, o0, o1)


def _moba(proj, B, S):
    nblk = S // MOBA_BLOCK
    n_hp = MOBA_HEADS // HEADS_PER_LANE_TILE
    cq, ck, cv = COL_BQ // LANES, COL_BK // LANES, COL_BV // LANES
    return pl.pallas_call(
        functools.partial(_moba_kernel, nblk=nblk),
        grid=(B, n_hp, nblk),
        in_specs=[
            pl.BlockSpec((1, MOBA_BLOCK, LANES), lambda b, h, n: (b, n, cq + h)),
            pl.BlockSpec((1, S, LANES), lambda b, h, n: (b, 0, ck + h)),
            pl.BlockSpec((1, S, LANES), lambda b, h, n: (b, 0, cv + h)),
        ],
        out_specs=pl.BlockSpec((1, MOBA_BLOCK, LANES), lambda b, h, n: (b, n, h)),
        out_shape=jax.ShapeDtypeStruct((B, S, D_B), F32),
        scratch_shapes=[pltpu.VMEM((nblk, LANES), F32)] * 3,
        compiler_params=pltpu.CompilerParams(
            dimension_semantics=("parallel", "parallel", "arbitrary"), vmem_limit_bytes=VMEM_LIMIT),
        name="moba",
    )(proj, proj, proj)


def _memkv_kernel(m_ref, nw_ref, w_ref, o_ref):
    xf = m_ref[...]
    ms = jnp.mean(xf * xf, axis=-1, keepdims=True)
    h = (xf * lax.rsqrt(ms + EPS) * nw_ref[...]).astype(BF16)
    o_ref[...] = jnp.dot(h, w_ref[...], preferred_element_type=F32).astype(BF16)


def _memkv(mem2, mem_norm_w, w_bf16):
    R = mem2.shape[0]
    return pl.pallas_call(
        _memkv_kernel,
        out_shape=jax.ShapeDtypeStruct((R, 2 * D_M), BF16),
        compiler_params=pltpu.CompilerParams(vmem_limit_bytes=VMEM_LIMIT),
        name="memkv",
    )(mem2, mem_norm_w, w_bf16)


def _silu(x):
    return x * jax.nn.sigmoid(x)


def _merge_kernel(x_ref, o1_ref, o2_ref, o3_ref, l1_ref, l2_ref, l3_ref, ob_ref, ga_ref, gb_ref,
                  mq_ref, mg_ref, lg_ref, kv_ref, bm_ref, wa_ref, wb_ref, wm_ref, wo_ref, fw_ref,
                  out_ref):
    tm = x_ref.shape[1]
    l1, l2, l3 = l1_ref[0], l2_ref[0], l3_ref[0]
    lmax = jnp.maximum(jnp.maximum(l1, l2), l3)
    e1, e2, e3 = jnp.exp(l1 - lmax), jnp.exp(l2 - lmax), jnp.exp(l3 - lmax)
    inv = 1.0 / (e1 + e2 + e3)
    erow = lax.broadcasted_iota(jnp.int32, (LANES, D_A), 0)
    ecol = lax.broadcasted_iota(jnp.int32, (LANES, D_A), 1)
    expand = jnp.where((ecol >= erow * HEAD_DIM) & (ecol < (erow + 1) * HEAD_DIM), 1.0, 0.0).astype(BF16)

    def per_head(w):
        hi, mid, lo = _split3(w)
        return (jnp.dot(hi, expand, preferred_element_type=F32)
                + jnp.dot(mid, expand, preferred_element_type=F32)
                + jnp.dot(lo, expand, preferred_element_type=F32))

    o_a = (per_head(e1 * inv) * o1_ref[0] + per_head(e2 * inv) * o2_ref[0]
           + per_head(e3 * inv) * o3_ref[0])
    t_a = (o_a * _silu(ga_ref[0].astype(F32))).astype(BF16)
    p_a = jnp.dot(t_a, wa_ref[...], preferred_element_type=F32)

    t_b = (ob_ref[0] * _silu(gb_ref[0].astype(F32))).astype(BF16)
    p_b = jnp.dot(t_b, wb_ref[...], preferred_element_type=F32)

    mq = mq_ref[0] * ATTN_SCALE
    km = kv_ref[0, :, :D_M]
    vm = kv_ref[0, :, D_M:]
    mlane = lax.broadcasted_iota(jnp.int32, (tm, D_M), 1)
    o_m = jnp.zeros((tm, D_M), F32)
    for h in range(MEM_HEADS):
        in_head = (mlane >= h * HEAD_DIM) & (mlane < (h + 1) * HEAD_DIM)
        s = _nt_dot(jnp.where(in_head, mq, jnp.zeros_like(mq)), km)
        s = s - jnp.max(s, axis=-1, keepdims=True)
        e = jnp.exp(s)
        p = (e / jnp.sum(e, axis=-1, keepdims=True)).astype(BF16)
        o_m = jnp.where(in_head, jnp.dot(p, vm, preferred_element_type=F32), o_m)
    t_m = (o_m * _silu(mg_ref[0].astype(F32))).astype(BF16)
    p_m = jnp.dot(t_m, wm_ref[...], preferred_element_type=F32)

    gates = jax.nn.sigmoid(lg_ref[0].astype(F32) + bm_ref[...])
    merged = (gates[:, :D_MODEL] * p_a + gates[:, D_MODEL:2 * D_MODEL] * p_b
              + gates[:, 2 * D_MODEL:] * p_m)
    y = x_ref[0] + jnp.dot(merged.astype(BF16), wo_ref[...], preferred_element_type=F32)
    ms = jnp.mean(y * y, axis=-1, keepdims=True)
    out_ref[0] = y * lax.rsqrt(ms + EPS) * fw_ref[...]


def _merge(x, o_groups, lse_groups, o_b, proj, kv, b_merge, wa, wb, wm, wo, fw, tm=256):
    B, S, _ = x.shape

    def tok(width, colblk):
        return pl.BlockSpec((1, tm, width), lambda b, i: (b, i, colblk))

    def full(shape):
        return pl.BlockSpec(shape, lambda b, i: (0,) * len(shape))

    in_specs = (
        [tok(D_MODEL, 0)]
        + [tok(D_A, 0)] * 3
        + [tok(LANES, 0)] * 3
        + [tok(D_B, 0)]
        + [tok(D_A, COL_A_GATE // D_A), tok(D_B, COL_B_GATE // D_B),
           tok(D_M, COL_MQ // D_M), tok(D_M, COL_M_GATE // D_M), tok(D_MERGE, COL_MERGE // D_MERGE)]
        + [pl.BlockSpec((1, kv.shape[1], 2 * D_M), lambda b, i: (b, 0, 0))]
        + [full((1, D_MERGE)), full((D_A, D_MODEL)), full((D_B, D_MODEL)), full((D_M, D_MODEL)),
           full((D_MODEL, D_MODEL)), full((1, D_MODEL))]
    )
    return pl.pallas_call(
        _merge_kernel,
        grid=(B, S // tm),
        in_specs=in_specs,
        out_specs=tok(D_MODEL, 0),
        out_shape=jax.ShapeDtypeStruct((B, S, D_MODEL), F32),
        compiler_params=pltpu.CompilerParams(
            dimension_semantics=("parallel", "parallel"), vmem_limit_bytes=VMEM_LIMIT),
        name="merge",
    )(x, *o_groups, *lse_groups, o_b, proj, proj, proj, proj, proj, kv, b_merge, wa, wb, wm, wo, fw)


def kernel(x, mem, norm_w, mem_norm_w, w_in, b_merge, w_mem_kv, w_branch_a, w_branch_b,
           w_branch_m, w_out, final_norm_w):
    B, S, D = x.shape
    assert w_in.shape[0] == 1, "single-layer block"
    n_merge_src = D_IN - D_MERGE
    for l in range(1):
        w_perm = jnp.concatenate([w_in[l][:, n_merge_src:], w_in[l][:, :n_merge_src]], axis=1).astype(BF16)
        proj = _inproj(x.reshape(B * S, D), norm_w[l][None, :], w_perm).reshape(B, S, D_IN)
        o_groups, lse_groups = [], []
        for g in range(N_GROUPS):
            o_g, lse_g = _dilated_group(proj, g, B, S)
            o_groups.append(o_g)
            lse_groups.append(lse_g)
        o_b = _moba(proj, B, S)
        kv = _memkv(mem.reshape(-1, D), mem_norm_w[l][None, :], w_mem_kv[l].astype(BF16))
        kv = kv.reshape(B, mem.shape[1], 2 * D_M)
        x = _merge(x, o_groups, lse_groups, o_b, proj, kv, b_merge[l][None, :],
                   w_branch_a[l].astype(BF16), w_branch_b[l].astype(BF16),
                   w_branch_m[l].astype(BF16), w_out[l].astype(BF16), final_norm_w[None, :])
    return x
```

```python
import functools

import jax
import jax.numpy as jnp
from jax import lax
from jax.experimental import pallas as pl
from jax.experimental.pallas import tpu as pltpu

F32 = jnp.float32
BF16 = jnp.bfloat16

D_MODEL = 1024
HEAD_DIM = 64
ATTN_SCALE = HEAD_DIM ** -0.5
DIL_PAIRS = ((128, 1), (512, 4), (2048, 16))
N_GROUPS = len(DIL_PAIRS)
DIL_HEADS = 8
DIL_QBLK = 128
MOBA_HEADS = 8
MOBA_BLOCK = 256
MOBA_TOPK = 3
MEM_HEADS = 4
N_BRANCHES = 3
EPS = 1e-6
NEG_INF = -1e30

D_A = DIL_HEADS * HEAD_DIM
D_B = MOBA_HEADS * HEAD_DIM
D_M = MEM_HEADS * HEAD_DIM
D_MERGE = N_BRANCHES * D_MODEL
D_IN = 3 * N_GROUPS * D_A + D_A + 4 * D_B + 2 * D_M + D_MERGE

COL_MERGE = 0
COL_A = D_MERGE
COL_A_GATE = COL_A + 3 * N_GROUPS * D_A
COL_BQ = COL_A_GATE + D_A
COL_BK = COL_BQ + D_B
COL_BV = COL_BK + D_B
COL_B_GATE = COL_BV + D_B
COL_MQ = COL_B_GATE + D_B
COL_M_GATE = COL_MQ + D_M

LANES = 128
HEADS_PER_LANE_TILE = LANES // HEAD_DIM

SLOPES_A = tuple(2.0 ** (-8.0 * k / 16) for k in range(1, 17, 2))
SLOPES_B = tuple(2.0 ** (-8.0 * k / 16) for k in range(2, 17, 2))

VMEM_LIMIT = 48 * 1024 * 1024


def _nt_dot(a, b):
    return lax.dot_general(a, b, (((1,), (1,)), ((), ())), preferred_element_type=F32)


def _split3(v):
    hi = v.astype(BF16)
    r1 = v - hi.astype(F32)
    mid = r1.astype(BF16)
    lo = (r1 - mid.astype(F32)).astype(BF16)
    return hi, mid, lo


def _inproj_kernel(x_ref, nw_ref, w_ref, o_ref, h_ref):
    @pl.when(pl.program_id(1) == 0)
    def _():
        xf = x_ref[...]
        ms = jnp.mean(xf * xf, axis=-1, keepdims=True)
        h_ref[...] = (xf * lax.rsqrt(ms + EPS) * nw_ref[...]).astype(BF16)

    o_ref[...] = jnp.dot(h_ref[...], w_ref[...], preferred_element_type=F32).astype(BF16)


def _inproj(x2, norm_w, w_bf16, tm=1024, tn=1536):
    T = x2.shape[0]
    return pl.pallas_call(
        _inproj_kernel,
        grid=(T // tm, D_IN // tn),
        in_specs=[
            pl.BlockSpec((tm, D_MODEL), lambda i, j: (i, 0)),
            pl.BlockSpec((1, D_MODEL), lambda i, j: (0, 0)),
            pl.BlockSpec((D_MODEL, tn), lambda i, j: (0, j)),
        ],
        out_specs=pl.BlockSpec((tm, tn), lambda i, j: (i, j)),
        out_shape=jax.ShapeDtypeStruct((T, D_IN), BF16),
        scratch_shapes=[pltpu.VMEM((tm, D_MODEL), BF16)],
        compiler_params=pltpu.CompilerParams(
            dimension_semantics=("parallel", "arbitrary"), vmem_limit_bytes=VMEM_LIMIT),
        name="inproj",
    )(x2, norm_w, w_bf16)


def _dilated_kernel(q_ref, kp_ref, kc_ref, vp_ref, vc_ref, o_ref, lse_ref, *, dilation, span):
    blk = pl.program_id(2)
    Q = DIL_QBLK
    row = lax.broadcasted_iota(jnp.int32, (Q, 2 * Q), 0)
    col = lax.broadcasted_iota(jnp.int32, (Q, 2 * Q), 1)
    steps = row + Q - col
    valid = (steps >= 0) & (steps <= span) & ((blk > 0) | (col >= Q))
    dist = (steps * dilation).astype(F32)
    lane = lax.broadcasted_iota(jnp.int32, (Q, LANES), 1)
    lse_tile = jnp.zeros((Q, LANES), F32)
    for h in range(DIL_HEADS):
        hs = slice(h * HEAD_DIM, (h + 1) * HEAD_DIM)
        q = q_ref[0, :, hs]
        k = jnp.concatenate([kp_ref[0, :, hs], kc_ref[0, :, hs]], axis=0)
        v = jnp.concatenate([vp_ref[0, :, hs], vc_ref[0, :, hs]], axis=0)
        s = _nt_dot(q, k) * ATTN_SCALE
        s = jnp.where(valid, s - SLOPES_A[h] * dist, NEG_INF)
        m = jnp.max(s, axis=-1, keepdims=True)
        e = jnp.exp(s - m)
        den = jnp.sum(e, axis=-1, keepdims=True)
        o = jnp.dot(e.astype(BF16), v, preferred_element_type=F32) / den
        o_ref[0, :, hs] = o
        lse_tile = jnp.where(lane == h, m + jnp.log(den), lse_tile)
    lse_ref[0] = lse_tile


def _dilated_group(proj, g, B, S):
    window, dilation = DIL_PAIRS[g]
    n = S // dilation
    nb = n // DIL_QBLK
    span = window // dilation
    pv = proj.reshape(B, n, dilation * D_IN)
    cpb = D_IN // D_A
    cq = COL_A // D_A + 3 * g

    def spec(piece, prev):
        if prev:
            return pl.BlockSpec((1, DIL_QBLK, D_A),
                                lambda b, r, i: (b, jnp.maximum(i - 1, 0), r * cpb + cq + piece))
        return pl.BlockSpec((1, DIL_QBLK, D_A), lambda b, r, i: (b, i, r * cpb + cq + piece))

    o, lse = pl.pallas_call(
        functools.partial(_dilated_kernel, dilation=dilation, span=span),
        grid=(B, dilation, nb),
        in_specs=[spec(0, False), spec(1, True), spec(1, False), spec(2, True), spec(2, False)],
        out_specs=[
            pl.BlockSpec((1, DIL_QBLK, D_A), lambda b, r, i: (b, i, r)),
            pl.BlockSpec((1, DIL_QBLK, LANES), lambda b, r, i: (b, i, r)),
        ],
        out_shape=[
            jax.ShapeDtypeStruct((B, n, dilation * D_A), F32),
            jax.ShapeDtypeStruct((B, n, dilation * LANES), F32),
        ],
        compiler_params=pltpu.CompilerParams(
            dimension_semantics=("parallel", "parallel", "arbitrary"), vmem_limit_bytes=VMEM_LIMIT),
        name=f"dilated_g{g}",
    )(pv, pv, pv, pv, pv)
    return o.reshape(B, S, D_A), lse.reshape(B, S, LANES)


MOBA_PAIR = 2


def _moba_kernel(q_ref, k_ref, v_ref, o_ref, kmh_ref, kmm_ref, kml_ref, vt_ref, q2_ref, cb_ref,
                 sel_ref, m_ref, acc_ref, r_ref, p_ref, alpha_ref, *, nblk):
    hp = pl.program_id(1)
    n = pl.program_id(2)
    BLK = MOBA_BLOCK
    NH = HEADS_PER_LANE_TILE
    PAIR = MOBA_PAIR
    ALIBI_LANE = (HEAD_DIM, 0)
    DEN_ROW = (HEAD_DIM, 0)

    @pl.when(n == 0)
    def _():
        vrow = lax.broadcasted_iota(jnp.int32, (LANES, BLK), 0)

        def body(j, c):
            rows = pl.ds(pl.multiple_of(j * BLK, BLK), BLK)
            km = jnp.sum(k_ref[0, rows, :].astype(F32), axis=0, keepdims=True) * (1.0 / BLK)
            hi, mid, lo = _split3(km)
            kmh_ref[pl.ds(j, 1), :] = hi.astype(F32)
            kmm_ref[pl.ds(j, 1), :] = mid.astype(F32)
            kml_ref[pl.ds(j, 1), :] = lo.astype(F32)
            vt = v_ref[0, rows, :].astype(F32).T
            for hh in range(NH):
                own = (vrow >= hh * HEAD_DIM) & (vrow < (hh + 1) * HEAD_DIM)
                ones = jnp.where(vrow == DEN_ROW[hh], 1.0, 0.0)
                vt_ref[j, hh] = jnp.where(own, vt, ones).astype(BF16)
            return c
        lax.fori_loop(0, nblk, body, 0)

    lane = lax.broadcasted_iota(jnp.int32, (BLK, LANES), 1)
    key_off = lax.broadcasted_iota(jnp.int32, (BLK, LANES), 0).astype(F32)
    brow = lax.broadcasted_iota(jnp.int32, (nblk, BLK), 0)
    brow_f = brow.astype(F32)
    kk = lax.broadcasted_iota(jnp.int32, (BLK, BLK), 0)
    tt = lax.broadcasted_iota(jnp.int32, (BLK, BLK), 1)

    q_all = q_ref[0] * ATTN_SCALE
    kmh = kmh_ref[...].astype(BF16)
    kmm = kmm_ref[...].astype(BF16)
    kml = kml_ref[...].astype(BF16)
    k_own = k_ref[0, pl.ds(pl.multiple_of(n * BLK, BLK), BLK), :]

    slopes, in_head = [], []
    for hh in range(NH):
        inh = (lane >= hh * HEAD_DIM) & (lane < (hh + 1) * HEAD_DIM)
        in_head.append(inh)
        is_alibi = lane == ALIBI_LANE[hh]
        slope = jnp.float32(SLOPES_B[hh])
        for p in range(1, MOBA_HEADS // NH):
            slope = jnp.where(hp == p, jnp.float32(SLOPES_B[NH * p + hh]), slope)
        slopes.append(slope)
        qm = jnp.where(inh, q_all, jnp.zeros_like(q_all))
        q2_ref[hh] = jnp.where(is_alibi, jnp.ones_like(q_all), qm)
        cb_ref[hh] = jnp.where(is_alibi, slope * key_off, 0.0).astype(BF16)
        g = _nt_dot(kmh, qm) + _nt_dot(kmm, qm) + _nt_dot(kml, qm)
        g = jnp.where(brow < n, g, NEG_INF)
        for t in range(MOBA_TOPK):
            mx = jnp.max(g, axis=0, keepdims=True)
            idx = jnp.min(jnp.where(g == mx, brow_f, float(nblk)), axis=0, keepdims=True)
            sel_ref[hh, pl.ds(t, 1), :] = jnp.where(t < n, idx, -1.0)
            g = jnp.where(brow_f == idx, -jnp.inf, g)
        r = _nt_dot(jnp.where(inh, k_own, cb_ref[hh]), q2_ref[hh])
        r = jnp.where(kk <= tt, r, NEG_INF)
        m = jnp.max(r, axis=0, keepdims=True)
        p = jnp.exp(r - m).astype(BF16)
        acc_ref[hh] = jnp.dot(vt_ref[n, hh], p, preferred_element_type=F32)
        m_ref[hh] = m

    r_ref[PAIR:] = jnp.zeros((PAIR, NH, BLK, BLK), F32)
    p_ref[:PAIR] = jnp.zeros((PAIR, NH, BLK, BLK), BF16)
    alpha_ref[:PAIR] = jnp.ones((PAIR, NH, 1, BLK), F32)
    last = jnp.maximum(n - 1, 0)

    def body(i, carry):
        s_new = (i & 1) * PAIR
        s_old = PAIR - s_new
        for hh in range(NH):
            acc = acc_ref[hh]
            for k in range(PAIR):
                c = jnp.clip(PAIR * (i - 2) + k, 0, last)
                acc = alpha_ref[s_new + k, hh] * acc + jnp.dot(
                    vt_ref[c, hh], p_ref[s_new + k, hh], preferred_element_type=F32)
            acc_ref[hh] = acc
        for hh in range(NH):
            m = m_ref[hh]
            for k in range(PAIR):
                b = PAIR * (i - 1) + k
                bf = jnp.where((b >= 0) & (b < n), b, -2).astype(F32)
                r = r_ref[s_old + k, hh]
                chosen = ((sel_ref[hh, 0:1, :] == bf) | (sel_ref[hh, 1:2, :] == bf)
                          | (sel_ref[hh, 2:3, :] == bf))
                off = jnp.where(chosen, -slopes[hh] * ((n - b) * BLK).astype(F32), NEG_INF)
                m_new = jnp.maximum(m, jnp.max(r, axis=0, keepdims=True) + off)
                alpha_ref[s_old + k, hh] = jnp.exp(m - m_new)
                p_ref[s_old + k, hh] = jnp.exp(r - (m_new - off)).astype(BF16)
                m = m_new
            m_ref[hh] = m
        for k in range(PAIR):
            a = jnp.minimum(PAIR * i + k, last)
            ka = k_ref[0, pl.ds(pl.multiple_of(a * BLK, BLK), BLK), :]
            for hh in range(NH):
                r_ref[s_new + k, hh] = _nt_dot(jnp.where(in_head[hh], ka, cb_ref[hh]), q2_ref[hh])
        return carry

    lax.fori_loop(0, jnp.where(n > 0, (n + PAIR - 1) // PAIR + 2, 0), body, 0)
    outs = []
    for hh in range(NH):
        a = acc_ref[hh]
        den = a[DEN_ROW[hh]:DEN_ROW[hh] + 1, :]
        outs.append(a[hh * HEAD_DIM:(hh + 1) * HEAD_DIM, :] / den)
    o_ref[0] = jnp.concatenate(outs, axis=0).T


def _moba(proj, B, S):
    nblk = S // MOBA_BLOCK
    nh = HEADS_PER_LANE_TILE
    n_hp = MOBA_HEADS // nh
    cq, ck, cv = COL_BQ // LANES, COL_BK // LANES, COL_BV // LANES
    return pl.pallas_call(
        functools.partial(_moba_kernel, nblk=nblk),
        grid=(B, n_hp, nblk),
        in_specs=[
            pl.BlockSpec((1, MOBA_BLOCK, LANES), lambda b, h, n: (b, n, cq + h)),
            pl.BlockSpec((1, S, LANES), lambda b, h, n: (b, 0, ck + h)),
            pl.BlockSpec((1, S, LANES), lambda b, h, n: (b, 0, cv + h)),
        ],
        out_specs=pl.BlockSpec((1, MOBA_BLOCK, LANES), lambda b, h, n: (b, n, h)),
        out_shape=jax.ShapeDtypeStruct((B, S, D_B), F32),
        scratch_shapes=[pltpu.VMEM((nblk, LANES), F32)] * 3 + [
            pltpu.VMEM((nblk, nh, LANES, MOBA_BLOCK), BF16),
            pltpu.VMEM((nh, MOBA_BLOCK, LANES), BF16),
            pltpu.VMEM((nh, MOBA_BLOCK, LANES), BF16),
            pltpu.VMEM((nh, 8, MOBA_BLOCK), F32),
            pltpu.VMEM((nh, 1, MOBA_BLOCK), F32),
            pltpu.VMEM((nh, LANES, MOBA_BLOCK), F32),
            pltpu.VMEM((2 * MOBA_PAIR, nh, MOBA_BLOCK, MOBA_BLOCK), F32),
            pltpu.VMEM((2 * MOBA_PAIR, nh, MOBA_BLOCK, MOBA_BLOCK), BF16),
            pltpu.VMEM((2 * MOBA_PAIR, nh, 1, MOBA_BLOCK), F32),
        ],
        compiler_params=pltpu.CompilerParams(
            dimension_semantics=("parallel", "parallel", "arbitrary"), vmem_limit_bytes=VMEM_LIMIT),
        name="moba",
    )(proj, proj, proj)


def _memkv_kernel(m_ref, nw_ref, w_ref, o_ref):
    xf = m_ref[...]
    ms = jnp.mean(xf * xf, axis=-1, keepdims=True)
    h = (xf * lax.rsqrt(ms + EPS) * nw_ref[...]).astype(BF16)
    o_ref[...] = jnp.dot(h, w_ref[...], preferred_element_type=F32).astype(BF16)


def _memkv(mem2, mem_norm_w, w_bf16):
    R = mem2.shape[0]
    return pl.pallas_call(
        _memkv_kernel,
        out_shape=jax.ShapeDtypeStruct((R, 2 * D_M), BF16),
        compiler_params=pltpu.CompilerParams(vmem_limit_bytes=VMEM_LIMIT),
        name="memkv",
    )(mem2, mem_norm_w, w_bf16)


def _silu(x):
    return x * jax.nn.sigmoid(x)


def _merge_kernel(x_ref, o1_ref, o2_ref, o3_ref, l1_ref, l2_ref, l3_ref, ob_ref, ga_ref, gb_ref,
                  mq_ref, mg_ref, lg_ref, kv_ref, bm_ref, wa_ref, wb_ref, wm_ref, wo_ref, fw_ref,
                  out_ref):
    tm = x_ref.shape[1]
    l1, l2, l3 = l1_ref[0], l2_ref[0], l3_ref[0]
    lmax = jnp.maximum(jnp.maximum(l1, l2), l3)
    e1, e2, e3 = jnp.exp(l1 - lmax), jnp.exp(l2 - lmax), jnp.exp(l3 - lmax)
    inv = 1.0 / (e1 + e2 + e3)
    erow = lax.broadcasted_iota(jnp.int32, (LANES, D_A), 0)
    ecol = lax.broadcasted_iota(jnp.int32, (LANES, D_A), 1)
    expand = jnp.where((ecol >= erow * HEAD_DIM) & (ecol < (erow + 1) * HEAD_DIM), 1.0, 0.0).astype(BF16)

    def per_head(w):
        hi, mid, lo = _split3(w)
        return (jnp.dot(hi, expand, preferred_element_type=F32)
                + jnp.dot(mid, expand, preferred_element_type=F32)
                + jnp.dot(lo, expand, preferred_element_type=F32))

    o_a = (per_head(e1 * inv) * o1_ref[0] + per_head(e2 * inv) * o2_ref[0]
           + per_head(e3 * inv) * o3_ref[0])
    t_a = (o_a * _silu(ga_ref[0].astype(F32))).astype(BF16)
    p_a = jnp.dot(t_a, wa_ref[...], preferred_element_type=F32)

    t_b = (ob_ref[0] * _silu(gb_ref[0].astype(F32))).astype(BF16)
    p_b = jnp.dot(t_b, wb_ref[...], preferred_element_type=F32)

    mq = mq_ref[0] * ATTN_SCALE
    km = kv_ref[0, :, :D_M]
    vm = kv_ref[0, :, D_M:]
    mlane = lax.broadcasted_iota(jnp.int32, (tm, D_M), 1)
    o_m = jnp.zeros((tm, D_M), F32)
    for h in range(MEM_HEADS):
        in_head = (mlane >= h * HEAD_DIM) & (mlane < (h + 1) * HEAD_DIM)
        s = _nt_dot(jnp.where(in_head, mq, jnp.zeros_like(mq)), km)
        s = s - jnp.max(s, axis=-1, keepdims=True)
        e = jnp.exp(s)
        p = (e / jnp.sum(e, axis=-1, keepdims=True)).astype(BF16)
        o_m = jnp.where(in_head, jnp.dot(p, vm, preferred_element_type=F32), o_m)
    t_m = (o_m * _silu(mg_ref[0].astype(F32))).astype(BF16)
    p_m = jnp.dot(t_m, wm_ref[...], preferred_element_type=F32)

    gates = jax.nn.sigmoid(lg_ref[0].astype(F32) + bm_ref[...])
    merged = (gates[:, :D_MODEL] * p_a + gates[:, D_MODEL:2 * D_MODEL] * p_b
              + gates[:, 2 * D_MODEL:] * p_m)
    y = x_ref[0] + jnp.dot(merged.astype(BF16), wo_ref[...], preferred_element_type=F32)
    ms = jnp.mean(y * y, axis=-1, keepdims=True)
    out_ref[0] = y * lax.rsqrt(ms + EPS) * fw_ref[...]


def _merge(x, o_groups, lse_groups, o_b, proj, kv, b_merge, wa, wb, wm, wo, fw, tm=256):
    B, S, _ = x.shape

    def tok(width, colblk):
        return pl.BlockSpec((1, tm, width), lambda b, i: (b, i, colblk))

    def full(shape):
        return pl.BlockSpec(shape, lambda b, i: (0,) * len(shape))

    in_specs = (
        [tok(D_MODEL, 0)]
        + [tok(D_A, 0)] * 3
        + [tok(LANES, 0)] * 3
        + [tok(D_B, 0)]
        + [tok(D_A, COL_A_GATE // D_A), tok(D_B, COL_B_GATE // D_B),
           tok(D_M, COL_MQ // D_M), tok(D_M, COL_M_GATE // D_M), tok(D_MERGE, COL_MERGE // D_MERGE)]
        + [pl.BlockSpec((1, kv.shape[1], 2 * D_M), lambda b, i: (b, 0, 0))]
        + [full((1, D_MERGE)), full((D_A, D_MODEL)), full((D_B, D_MODEL)), full((D_M, D_MODEL)),
           full((D_MODEL, D_MODEL)), full((1, D_MODEL))]
    )
    return pl.pallas_call(
        _merge_kernel,
        grid=(B, S // tm),
        in_specs=in_specs,
        out_specs=tok(D_MODEL, 0),
        out_shape=jax.ShapeDtypeStruct((B, S, D_MODEL), F32),
        compiler_params=pltpu.CompilerParams(
            dimension_semantics=("parallel", "parallel"), vmem_limit_bytes=VMEM_LIMIT),
        name="merge",
    )(x, *o_groups, *lse_groups, o_b, proj, proj, proj, proj, proj, kv, b_merge, wa, wb, wm, wo, fw)


def kernel(x, mem, norm_w, mem_norm_w, w_in, b_merge, w_mem_kv, w_branch_a, w_branch_b,
           w_branch_m, w_out, final_norm_w):
    B, S, D = x.shape
    assert w_in.shape[0] == 1, "single-layer block"
    n_merge_src = D_IN - D_MERGE
    for l in range(1):
        w_perm = jnp.concatenate([w_in[l][:, n_merge_src:], w_in[l][:, :n_merge_src]], axis=1).astype(BF16)
        proj = _inproj(x.reshape(B * S, D), norm_w[l][None, :], w_perm).reshape(B, S, D_IN)
        o_groups, lse_groups = [], []
        for g in range(N_GROUPS):
            o_g, lse_g = _dilated_group(proj, g, B, S)
            o_groups.append(o_g)
            lse_groups.append(lse_g)
        o_b = _moba(proj, B, S)
        kv = _memkv(mem.reshape(-1, D), mem_norm_w[l][None, :], w_mem_kv[l].astype(BF16))
        kv = kv.reshape(B, mem.shape[1], 2 * D_M)
        x = _merge(x, o_groups, lse_groups, o_b, proj, kv, b_merge[l][None, :],
                   w_branch_a[l].astype(BF16), w_branch_b[l].astype(BF16),
                   w_branch_m[l].astype(BF16), w_out[l].astype(BF16), final_norm_w[None, :])
    return x
```

```python
import functools
import math

import jax
import jax.numpy as jnp
from jax import lax
from jax.experimental import pallas as pl
from jax.experimental.pallas import tpu as pltpu

F32 = jnp.float32
BF16 = jnp.bfloat16

D_MODEL = 1024
HEAD_DIM = 64
ATTN_SCALE = HEAD_DIM ** -0.5
LOG2E = math.log2(math.e)
DIL_PAIRS = ((128, 1), (512, 4), (2048, 16))
N_GROUPS = len(DIL_PAIRS)
DIL_HEADS = 8
DIL_QBLK = 128
MOBA_HEADS = 8
MOBA_BLOCK = 256
MOBA_TOPK = 3
MEM_HEADS = 4
N_BRANCHES = 3
EPS = 1e-6
NEG_INF = -1e30

D_A = DIL_HEADS * HEAD_DIM
D_B = MOBA_HEADS * HEAD_DIM
D_M = MEM_HEADS * HEAD_DIM
D_MERGE = N_BRANCHES * D_MODEL
D_QKV = 3 * D_A

COL_MERGE = 0
COL_A0 = D_MERGE
COL_A_GATE = COL_A0 + D_QKV
COL_BQ = COL_A_GATE + D_A
COL_BK = COL_BQ + D_B
COL_BV = COL_BK + D_B
COL_B_GATE = COL_BV + D_B
COL_MQ = COL_B_GATE + D_B
COL_M_GATE = COL_MQ + D_M
D_MAIN = COL_M_GATE + D_M

LANES = 128
HEADS_PER_LANE_TILE = LANES // HEAD_DIM

SLOPES_A = tuple(2.0 ** (-8.0 * k / 16) for k in range(1, 17, 2))
SLOPES_B = tuple(2.0 ** (-8.0 * k / 16) for k in range(2, 17, 2))

VMEM_LIMIT = 48 * 1024 * 1024


def _nt_dot(a, b):
    return lax.dot_general(a, b, (((1,), (1,)), ((), ())), preferred_element_type=F32)


def _split3(v):
    hi = v.astype(BF16)
    r1 = v - hi.astype(F32)
    mid = r1.astype(BF16)
    lo = (r1 - mid.astype(F32)).astype(BF16)
    return hi, mid, lo


def _rmsnorm_bf16(xf, w):
    ms = jnp.mean(xf * xf, axis=-1, keepdims=True)
    return (xf * lax.rsqrt(ms + EPS) * w).astype(BF16)


def _inproj_kernel(x_ref, nw_ref, w_ref, o_ref, h_ref):
    @pl.when(pl.program_id(1) == 0)
    def _():
        h_ref[...] = _rmsnorm_bf16(x_ref[...], nw_ref[...])

    o_ref[...] = jnp.dot(h_ref[...], w_ref[...], preferred_element_type=F32).astype(BF16)


def _inproj(x2, norm_w, w_bf16, name, tm=1024, tn=1536):
    T = x2.shape[0]
    n_cols = w_bf16.shape[1]
    return pl.pallas_call(
        _inproj_kernel,
        grid=(T // tm, n_cols // tn),
        in_specs=[
            pl.BlockSpec((tm, D_MODEL), lambda i, j: (i, 0)),
            pl.BlockSpec((1, D_MODEL), lambda i, j: (0, 0)),
            pl.BlockSpec((D_MODEL, tn), lambda i, j: (0, j)),
        ],
        out_specs=pl.BlockSpec((tm, tn), lambda i, j: (i, j)),
        out_shape=jax.ShapeDtypeStruct((T, n_cols), BF16),
        scratch_shapes=[pltpu.VMEM((tm, D_MODEL), BF16)],
        compiler_params=pltpu.CompilerParams(
            dimension_semantics=("parallel", "arbitrary"), vmem_limit_bytes=VMEM_LIMIT),
        name=name,
    )(x2, norm_w, w_bf16)


def _dilated_kernel(q_ref, kp_ref, kc_ref, vp_ref, vc_ref, o_ref, lse_ref, *, dilation, span):
    blk = pl.program_id(2)
    Q = DIL_QBLK
    row = lax.broadcasted_iota(jnp.int32, (Q, 2 * Q), 0)
    col = lax.broadcasted_iota(jnp.int32, (Q, 2 * Q), 1)
    steps = row + Q - col
    valid = (steps >= 0) & (steps <= span) & ((blk > 0) | (col >= Q))
    dist = (steps * dilation).astype(F32)
    lane = lax.broadcasted_iota(jnp.int32, (Q, LANES), 1)
    lse_tile = jnp.zeros((Q, LANES), F32)
    for h in range(DIL_HEADS):
        hs = slice(h * HEAD_DIM, (h + 1) * HEAD_DIM)
        q = q_ref[0, 0, :, hs]
        k = jnp.concatenate([kp_ref[0, 0, :, hs], kc_ref[0, 0, :, hs]], axis=0)
        v = jnp.concatenate([vp_ref[0, 0, :, hs], vc_ref[0, 0, :, hs]], axis=0)
        s = _nt_dot(q, k) * ATTN_SCALE
        s = jnp.where(valid, s - SLOPES_A[h] * dist, NEG_INF)
        m = jnp.max(s, axis=-1, keepdims=True)
        e = jnp.exp(s - m)
        den = jnp.sum(e, axis=-1, keepdims=True)
        o = jnp.dot(e.astype(BF16), v, preferred_element_type=F32) / den
        o_ref[0, :, hs] = o
        lse_tile = jnp.where(lane == h, m + jnp.log(den), lse_tile)
    lse_ref[0] = lse_tile


def _dilated_group(qkv, col0, g, B, S):
    window, dilation = DIL_PAIRS[g]
    n = S // dilation
    nb = n // DIL_QBLK
    span = window // dilation

    def spec(piece, prev):
        if prev:
            return pl.BlockSpec((1, 1, DIL_QBLK, D_A),
                                lambda b, r, i: (b, r, jnp.maximum(i - 1, 0), col0 + piece))
        return pl.BlockSpec((1, 1, DIL_QBLK, D_A), lambda b, r, i: (b, r, i, col0 + piece))

    o, lse = pl.pallas_call(
        functools.partial(_dilated_kernel, dilation=dilation, span=span),
        grid=(B, dilation, nb),
        in_specs=[spec(0, False), spec(1, True), spec(1, False), spec(2, True), spec(2, False)],
        out_specs=[
            pl.BlockSpec((1, DIL_QBLK, D_A), lambda b, r, i: (b, i, r)),
            pl.BlockSpec((1, DIL_QBLK, LANES), lambda b, r, i: (b, i, r)),
        ],
        out_shape=[
            jax.ShapeDtypeStruct((B, n, dilation * D_A), F32),
            jax.ShapeDtypeStruct((B, n, dilation * LANES), F32),
        ],
        compiler_params=pltpu.CompilerParams(
            dimension_semantics=("parallel", "parallel", "arbitrary"), vmem_limit_bytes=VMEM_LIMIT),
        name=f"dilated_g{g}",
    )(qkv, qkv, qkv, qkv, qkv)
    return o.reshape(B, S, D_A), lse.reshape(B, S, LANES)


MOBA_PAIR = 2
MOBA_VROWS = HEAD_DIM + 16
SLOPES_B_LOG2 = tuple(s * LOG2E for s in SLOPES_B)


def _moba_kernel(q_ref, k_ref, v_ref, o_ref, kmh_ref, kmm_ref, kml_ref, k2_ref, vt_ref, q2t_ref,
                 sel_ref, m_ref, acc_ref, ra_ref, rb_ref, pa_ref, pb_ref, aa_ref, ab_ref, *, nblk):
    hp = pl.program_id(1)
    n = pl.program_id(2)
    BLK = MOBA_BLOCK
    NH = HEADS_PER_LANE_TILE
    PAIR = MOBA_PAIR
    ALIBI_LANE = (HEAD_DIM, 0)

    lane = lax.broadcasted_iota(jnp.int32, (BLK, LANES), 1)
    slopes, in_head, alibi_sel = [], [], []
    for hh in range(NH):
        in_head.append((lane >= hh * HEAD_DIM) & (lane < (hh + 1) * HEAD_DIM))
        alibi_sel.append([lane == ALIBI_LANE[hh] + t for t in range(3)])
        slope = jnp.float32(SLOPES_B_LOG2[hh])
        for p in range(1, MOBA_HEADS // NH):
            slope = jnp.where(hp == p, jnp.float32(SLOPES_B_LOG2[NH * p + hh]), slope)
        slopes.append(slope)

    @pl.when(n == 0)
    def _():
        key_off = lax.broadcasted_iota(jnp.int32, (BLK, LANES), 0).astype(F32)
        ones_tile = jnp.where(lax.broadcasted_iota(jnp.int32, (16, BLK), 0) == 0, 1.0, 0.0)
        cbs = []
        for hh in range(NH):
            terms = _split3(slopes[hh] * key_off)
            cb = jnp.zeros((BLK, LANES), F32)
            for t in range(3):
                cb = jnp.where(alibi_sel[hh][t], terms[t].astype(F32), cb)
            cbs.append(cb)

        def body(j, c):
            rows = pl.ds(pl.multiple_of(j * BLK, BLK), BLK)
            kb = k_ref[0, rows, :].astype(F32)
            km = jnp.sum(kb, axis=0, keepdims=True) * (1.0 / BLK)
            hi, mid, lo = _split3(km)
            kmh_ref[pl.ds(j, 1), :] = hi.astype(F32)
            kmm_ref[pl.ds(j, 1), :] = mid.astype(F32)
            kml_ref[pl.ds(j, 1), :] = lo.astype(F32)
            vt = v_ref[0, rows, :].astype(F32).T
            for hh in range(NH):
                k2_ref[j, hh] = jnp.where(in_head[hh], kb, cbs[hh]).astype(BF16)
                vh = vt[hh * HEAD_DIM:(hh + 1) * HEAD_DIM, :]
                vt_ref[j, hh] = jnp.concatenate([vh, ones_tile], axis=0).astype(BF16)
            return c
        lax.fori_loop(0, nblk, body, 0)
        rb_ref[...] = jnp.zeros((PAIR, NH, BLK, BLK), F32)
        pa_ref[...] = jnp.zeros((PAIR, NH, BLK, BLK), BF16)
        aa_ref[...] = jnp.ones((PAIR, NH, 1, BLK), F32)

    brow = lax.broadcasted_iota(jnp.int32, (nblk, NH * BLK), 0)
    brow_f = brow.astype(F32)
    kk = lax.broadcasted_iota(jnp.int32, (BLK, BLK), 0)
    tt = lax.broadcasted_iota(jnp.int32, (BLK, BLK), 1)

    q_all = q_ref[0]
    kmh = kmh_ref[...].astype(BF16)
    kmm = kmm_ref[...].astype(BF16)
    kml = kml_ref[...].astype(BF16)

    gates = []
    for hh in range(NH):
        qm = jnp.where(in_head[hh], q_all, jnp.zeros_like(q_all))
        is_alibi = alibi_sel[hh][0] | alibi_sel[hh][1] | alibi_sel[hh][2]
        q2t_ref[hh] = jnp.where(is_alibi, 1.0, qm.astype(F32)).T.astype(BF16)
        gates.append(_nt_dot(kmh, qm) + _nt_dot(kmm, qm) + _nt_dot(kml, qm))
    g = jnp.where(brow < n, jnp.concatenate(gates, axis=1), NEG_INF)
    for t in range(MOBA_TOPK):
        mx = jnp.max(g, axis=0, keepdims=True)
        idx = jnp.min(jnp.where(g == mx, brow_f, float(nblk)), axis=0, keepdims=True)
        sel_ref[pl.ds(t, 1), :] = jnp.where(t < n, idx, -1.0)
        g = jnp.where(brow_f == idx, -jnp.inf, g)

    for hh in range(NH):
        ra_ref[0, hh] = jnp.dot(k2_ref[n, hh], q2t_ref[hh], preferred_element_type=F32)
    for hh in range(NH):
        r = jnp.where(kk <= tt, ra_ref[0, hh], NEG_INF)
        m = jnp.max(r, axis=0, keepdims=True)
        pb_ref[0, hh] = jnp.exp2(r - m).astype(BF16)
        m_ref[hh] = m
    for hh in range(NH):
        acc_ref[hh] = jnp.dot(vt_ref[n, hh], pb_ref[0, hh], preferred_element_type=F32)

    last = jnp.maximum(n - 1, 0)

    def step(i, r_new, r_old, p_new, p_old, al_new, al_old):
        def scores():
            for k in range(PAIR):
                a = jnp.minimum(PAIR * i + k, last)
                for hh in range(NH):
                    r_new[k, hh] = jnp.dot(k2_ref[a, hh], q2t_ref[hh], preferred_element_type=F32)

        def pv():
            for hh in range(NH):
                acc = acc_ref[hh]
                for k in range(PAIR):
                    c = jnp.clip(PAIR * (i - 2) + k, 0, last)
                    acc = al_new[k, hh] * acc + jnp.dot(vt_ref[c, hh], p_new[k, hh],
                                                        preferred_element_type=F32)
                acc_ref[hh] = acc

        def softmax():
            for hh in range(NH):
                m = m_ref[hh]
                for k in range(PAIR):
                    b = PAIR * (i - 1) + k
                    bf = jnp.where((b >= 0) & (b < n), b, -2).astype(F32)
                    r = r_old[k, hh]
                    qs = slice(hh * BLK, (hh + 1) * BLK)
                    chosen = ((sel_ref[0:1, qs] == bf) | (sel_ref[1:2, qs] == bf)
                              | (sel_ref[2:3, qs] == bf))
                    off = jnp.where(chosen, -slopes[hh] * ((n - b) * BLK).astype(F32), NEG_INF)
                    m_new = jnp.maximum(m, jnp.max(r, axis=0, keepdims=True) + off)
                    al_old[k, hh] = jnp.exp2(m - m_new)
                    p_old[k, hh] = jnp.exp2(r - (m_new - off)).astype(BF16)
                    m = m_new
                m_ref[hh] = m

        for stage in (pv, softmax, scores):
            stage()

    def body(t, carry):
        step(2 * t, ra_ref, rb_ref, pa_ref, pb_ref, aa_ref, ab_ref)
        step(2 * t + 1, rb_ref, ra_ref, pb_ref, pa_ref, ab_ref, aa_ref)
        return carry

    n_steps = (n + PAIR - 1) // PAIR + 2
    lax.fori_loop(0, jnp.where(n > 0, (n_steps + 1) // 2, 0), body, 0)
    outs = []
    for hh in range(NH):
        a = acc_ref[hh]
        outs.append(a[:HEAD_DIM, :] / a[HEAD_DIM:HEAD_DIM + 1, :])
    o_ref[0] = jnp.concatenate(outs, axis=0).T


def _moba(proj, B, S):
    nblk = S // MOBA_BLOCK
    nh = HEADS_PER_LANE_TILE
    n_hp = MOBA_HEADS // nh
    cq, ck, cv = COL_BQ // LANES, COL_BK // LANES, COL_BV // LANES
    return pl.pallas_call(
        functools.partial(_moba_kernel, nblk=nblk),
        grid=(B, n_hp, nblk),
        in_specs=[
            pl.BlockSpec((1, MOBA_BLOCK, LANES), lambda b, h, n: (b, n, cq + h)),
            pl.BlockSpec((1, S, LANES), lambda b, h, n: (b, 0, ck + h), pipeline_mode=pl.Buffered(1)),
            pl.BlockSpec((1, S, LANES), lambda b, h, n: (b, 0, cv + h), pipeline_mode=pl.Buffered(1)),
        ],
        out_specs=pl.BlockSpec((1, MOBA_BLOCK, LANES), lambda b, h, n: (b, n, h)),
        out_shape=jax.ShapeDtypeStruct((B, S, D_B), F32),
        scratch_shapes=[pltpu.VMEM((nblk, LANES), F32)] * 3 + [
            pltpu.VMEM((nblk, nh, MOBA_BLOCK, LANES), BF16),
            pltpu.VMEM((nblk, nh, MOBA_VROWS, MOBA_BLOCK), BF16),
            pltpu.VMEM((nh, LANES, MOBA_BLOCK), BF16),
            pltpu.VMEM((8, nh * MOBA_BLOCK), F32),
            pltpu.VMEM((nh, 1, MOBA_BLOCK), F32),
            pltpu.VMEM((nh, MOBA_VROWS, MOBA_BLOCK), F32),
        ] + [pltpu.VMEM((MOBA_PAIR, nh, MOBA_BLOCK, MOBA_BLOCK), F32)] * 2
        + [pltpu.VMEM((MOBA_PAIR, nh, MOBA_BLOCK, MOBA_BLOCK), BF16)] * 2
        + [pltpu.VMEM((MOBA_PAIR, nh, 1, MOBA_BLOCK), F32)] * 2,
        compiler_params=pltpu.CompilerParams(
            dimension_semantics=("parallel", "parallel", "arbitrary"), vmem_limit_bytes=VMEM_LIMIT),
        name="moba",
    )(proj, proj, proj)


def _memkv_kernel(m_ref, nw_ref, w_ref, o_ref):
    h = _rmsnorm_bf16(m_ref[...], nw_ref[...])
    o_ref[...] = jnp.dot(h, w_ref[...], preferred_element_type=F32).astype(BF16)


def _memkv(mem2, mem_norm_w, w_bf16):
    R = mem2.shape[0]
    return pl.pallas_call(
        _memkv_kernel,
        out_shape=jax.ShapeDtypeStruct((R, 2 * D_M), BF16),
        compiler_params=pltpu.CompilerParams(vmem_limit_bytes=VMEM_LIMIT),
        name="memkv",
    )(mem2, mem_norm_w, w_bf16)


def _silu(x):
    return x * jax.nn.sigmoid(x)


def _merge_kernel(x_ref, o1_ref, o2_ref, o3_ref, l1_ref, l2_ref, l3_ref, ob_ref, ga_ref, gb_ref,
                  mq_ref, mg_ref, lg_ref, kv_ref, bm_ref, wa_ref, wb_ref, wm_ref, wo_ref, fw_ref,
                  out_ref):
    tm = x_ref.shape[1]
    l1, l2, l3 = l1_ref[0], l2_ref[0], l3_ref[0]
    lmax = jnp.maximum(jnp.maximum(l1, l2), l3)
    e1, e2, e3 = jnp.exp(l1 - lmax), jnp.exp(l2 - lmax), jnp.exp(l3 - lmax)
    inv = 1.0 / (e1 + e2 + e3)
    erow = lax.broadcasted_iota(jnp.int32, (LANES, D_A), 0)
    ecol = lax.broadcasted_iota(jnp.int32, (LANES, D_A), 1)
    expand = jnp.where((ecol >= erow * HEAD_DIM) & (ecol < (erow + 1) * HEAD_DIM), 1.0, 0.0).astype(BF16)

    def per_head(w):
        hi, mid, lo = _split3(w)
        return (jnp.dot(hi, expand, preferred_element_type=F32)
                + jnp.dot(mid, expand, preferred_element_type=F32)
                + jnp.dot(lo, expand, preferred_element_type=F32))

    o_a = (per_head(e1 * inv) * o1_ref[0] + per_head(e2 * inv) * o2_ref[0]
           + per_head(e3 * inv) * o3_ref[0])
    t_a = (o_a * _silu(ga_ref[0].astype(F32))).astype(BF16)
    p_a = jnp.dot(t_a, wa_ref[...], preferred_element_type=F32)

    t_b = (ob_ref[0] * _silu(gb_ref[0].astype(F32))).astype(BF16)
    p_b = jnp.dot(t_b, wb_ref[...], preferred_element_type=F32)

    mq = mq_ref[0] * ATTN_SCALE
    km = kv_ref[0, :, :D_M]
    vm = kv_ref[0, :, D_M:]
    mlane = lax.broadcasted_iota(jnp.int32, (tm, D_M), 1)
    o_m = jnp.zeros((tm, D_M), F32)
    for h in range(MEM_HEADS):
        in_head = (mlane >= h * HEAD_DIM) & (mlane < (h + 1) * HEAD_DIM)
        s = _nt_dot(jnp.where(in_head, mq, jnp.zeros_like(mq)), km)
        s = s - jnp.max(s, axis=-1, keepdims=True)
        e = jnp.exp(s)
        p = (e / jnp.sum(e, axis=-1, keepdims=True)).astype(BF16)
        o_m = jnp.where(in_head, jnp.dot(p, vm, preferred_element_type=F32), o_m)
    t_m = (o_m * _silu(mg_ref[0].astype(F32))).astype(BF16)
    p_m = jnp.dot(t_m, wm_ref[...], preferred_element_type=F32)

    gates = jax.nn.sigmoid(lg_ref[0].astype(F32) + bm_ref[...])
    merged = (gates[:, :D_MODEL] * p_a + gates[:, D_MODEL:2 * D_MODEL] * p_b
              + gates[:, 2 * D_MODEL:] * p_m)
    y = x_ref[0] + jnp.dot(merged.astype(BF16), wo_ref[...], preferred_element_type=F32)
    ms = jnp.mean(y * y, axis=-1, keepdims=True)
    out_ref[0] = y * lax.rsqrt(ms + EPS) * fw_ref[...]


def _merge(x, o_groups, lse_groups, o_b, proj, kv, b_merge, wa, wb, wm, wo, fw, tm=256):
    B, S, _ = x.shape

    def tok(width, colblk):
        return pl.BlockSpec((1, tm, width), lambda b, i: (b, i, colblk))

    def full(shape):
        return pl.BlockSpec(shape, lambda b, i: (0,) * len(shape))

    in_specs = (
        [tok(D_MODEL, 0)]
        + [tok(D_A, 0)] * 3
        + [tok(LANES, 0)] * 3
        + [tok(D_B, 0)]
        + [tok(D_A, COL_A_GATE // D_A), tok(D_B, COL_B_GATE // D_B),
           tok(D_M, COL_MQ // D_M), tok(D_M, COL_M_GATE // D_M), tok(D_MERGE, COL_MERGE // D_MERGE)]
        + [pl.BlockSpec((1, kv.shape[1], 2 * D_M), lambda b, i: (b, 0, 0))]
        + [full((1, D_MERGE)), full((D_A, D_MODEL)), full((D_B, D_MODEL)), full((D_M, D_MODEL)),
           full((D_MODEL, D_MODEL)), full((1, D_MODEL))]
    )
    return pl.pallas_call(
        _merge_kernel,
        grid=(B, S // tm),
        in_specs=in_specs,
        out_specs=tok(D_MODEL, 0),
        out_shape=jax.ShapeDtypeStruct((B, S, D_MODEL), F32),
        compiler_params=pltpu.CompilerParams(
            dimension_semantics=("parallel", "parallel"), vmem_limit_bytes=VMEM_LIMIT),
        name="merge",
    )(x, *o_groups, *lse_groups, o_b, proj, proj, proj, proj, proj, kv, b_merge, wa, wb, wm, wo, fw)


def kernel(x, mem, norm_w, mem_norm_w, w_in, b_merge, w_mem_kv, w_branch_a, w_branch_b,
           w_branch_m, w_out, final_norm_w):
    B, S, D = x.shape
    assert w_in.shape[0] == 1, "single-layer block"
    w = w_in[0]
    nw = norm_w[0][None, :]
    src_a_gate = N_GROUPS * D_QKV
    src_bq = src_a_gate + D_A
    src_bk = src_bq + D_B
    src_merge = src_bk + 3 * D_B + 2 * D_M
    w_bq = w[:, src_bq:src_bk] * (ATTN_SCALE * LOG2E)
    w_main = jnp.concatenate(
        [w[:, src_merge:], w[:, :D_QKV], w[:, src_a_gate:src_bq], w_bq, w[:, src_bk:src_merge]],
        axis=1).astype(BF16)
    proj = _inproj(x.reshape(B * S, D), nw, w_main, "inproj").reshape(B, S, D_MAIN)

    o_groups, lse_groups = [], []
    for g, (_, dilation) in enumerate(DIL_PAIRS):
        if dilation == 1:
            qkv, col0 = proj.reshape(B, 1, S, D_MAIN), COL_A0 // D_A
        else:
            x_g = x.reshape(B, S // dilation, dilation, D).transpose(0, 2, 1, 3).reshape(B * S, D)
            w_g = w[:, g * D_QKV:(g + 1) * D_QKV].astype(BF16)
            qkv = _inproj(x_g, nw, w_g, f"inproj_d{dilation}").reshape(B, dilation, S // dilation, D_QKV)
            col0 = 0
        o_g, lse_g = _dilated_group(qkv, col0, g, B, S)
        o_groups.append(o_g)
        lse_groups.append(lse_g)
    o_b = _moba(proj, B, S)
    kv = _memkv(mem.reshape(-1, D), mem_norm_w[0][None, :], w_mem_kv[0].astype(BF16))
    kv = kv.reshape(B, mem.shape[1], 2 * D_M)
    return _merge(x, o_groups, lse_groups, o_b, proj, kv, b_merge[0][None, :],
                  w_branch_a[0].astype(BF16), w_branch_b[0].astype(BF16),
                  w_branch_m[0].astype(BF16), w_out[0].astype(BF16), final_norm_w[None, :])
```

```python
import functools
import math

import jax
import jax.numpy as jnp
from jax import lax
from jax.experimental import pallas as pl
from jax.experimental.pallas import tpu as pltpu

F32 = jnp.float32
BF16 = jnp.bfloat16

D_MODEL = 1024
HEAD_DIM = 64
ATTN_SCALE = HEAD_DIM ** -0.5
LOG2E = math.log2(math.e)
DIL_PAIRS = ((128, 1), (512, 4), (2048, 16))
N_GROUPS = len(DIL_PAIRS)
DIL_HEADS = 8
DIL_QBLK = 128
MOBA_HEADS = 8
MOBA_BLOCK = 256
MOBA_TOPK = 3
MEM_HEADS = 4
N_BRANCHES = 3
EPS = 1e-6
NEG_INF = -1e30

D_A = DIL_HEADS * HEAD_DIM
D_B = MOBA_HEADS * HEAD_DIM
D_M = MEM_HEADS * HEAD_DIM
D_MERGE = N_BRANCHES * D_MODEL
D_QKV = 3 * D_A

COL_MERGE = 0
COL_A0 = D_MERGE
COL_A_GATE = COL_A0 + D_QKV
COL_BQ = COL_A_GATE + D_A
COL_BK = COL_BQ + D_B
COL_BV = COL_BK + D_B
COL_B_GATE = COL_BV + D_B
COL_MQ = COL_B_GATE + D_B
COL_M_GATE = COL_MQ + D_M
D_MAIN = COL_M_GATE + D_M

LANES = 128
HEADS_PER_LANE_TILE = LANES // HEAD_DIM

SLOPES_A = tuple(2.0 ** (-8.0 * k / 16) for k in range(1, 17, 2))
SLOPES_B = tuple(2.0 ** (-8.0 * k / 16) for k in range(2, 17, 2))

VMEM_LIMIT = 48 * 1024 * 1024


def _nt_dot(a, b):
    return lax.dot_general(a, b, (((1,), (1,)), ((), ())), preferred_element_type=F32)


def _split3(v):
    hi = v.astype(BF16)
    r1 = v - hi.astype(F32)
    mid = r1.astype(BF16)
    lo = (r1 - mid.astype(F32)).astype(BF16)
    return hi, mid, lo


def _rmsnorm_bf16(xf, w):
    ms = jnp.mean(xf * xf, axis=-1, keepdims=True)
    return (xf * lax.rsqrt(ms + EPS) * w).astype(BF16)


def _inproj_kernel(x_ref, nw_ref, w_ref, o_ref, h_ref):
    @pl.when(pl.program_id(1) == 0)
    def _():
        h_ref[...] = _rmsnorm_bf16(x_ref[...], nw_ref[...])

    o_ref[...] = jnp.dot(h_ref[...], w_ref[...], preferred_element_type=F32).astype(BF16)


def _inproj(x2, norm_w, w_bf16, name, tm=1024, tn=1536):
    T = x2.shape[0]
    n_cols = w_bf16.shape[1]
    return pl.pallas_call(
        _inproj_kernel,
        grid=(T // tm, n_cols // tn),
        in_specs=[
            pl.BlockSpec((tm, D_MODEL), lambda i, j: (i, 0)),
            pl.BlockSpec((1, D_MODEL), lambda i, j: (0, 0)),
            pl.BlockSpec((D_MODEL, tn), lambda i, j: (0, j)),
        ],
        out_specs=pl.BlockSpec((tm, tn), lambda i, j: (i, j)),
        out_shape=jax.ShapeDtypeStruct((T, n_cols), BF16),
        scratch_shapes=[pltpu.VMEM((tm, D_MODEL), BF16)],
        compiler_params=pltpu.CompilerParams(
            dimension_semantics=("parallel", "arbitrary"), vmem_limit_bytes=VMEM_LIMIT),
        name=name,
    )(x2, norm_w, w_bf16)


def _dilated_kernel(q_ref, kp_ref, kc_ref, vp_ref, vc_ref, o_ref, lse_ref, bias_ref, s_ref, p_ref,
                    *, dilation, span):
    blk = pl.program_id(2)
    Q = DIL_QBLK
    NH = HEADS_PER_LANE_TILE
    n_tiles = DIL_HEADS // NH

    @pl.when((pl.program_id(0) == 0) & (pl.program_id(1) == 0) & (blk == 0))
    def _():
        row = lax.broadcasted_iota(jnp.int32, (Q, 2 * Q), 0)
        col = lax.broadcasted_iota(jnp.int32, (Q, 2 * Q), 1)
        steps = row + Q - col
        in_band = (steps >= 0) & (steps <= span)
        dist = (steps * dilation).astype(F32)
        for h in range(DIL_HEADS):
            bias = jnp.where(in_band, -SLOPES_A[h] * dist, NEG_INF)
            bias_ref[0, h] = bias
            bias_ref[1, h] = jnp.where(col >= Q, bias, NEG_INF)

    lane = lax.broadcasted_iota(jnp.int32, (Q, LANES), 1)
    in_head = [(lane >= hh * HEAD_DIM) & (lane < (hh + 1) * HEAD_DIM) for hh in range(NH)]
    first = (blk == 0).astype(jnp.int32)

    for t in range(n_tiles):
        ls = slice(t * LANES, (t + 1) * LANES)
        q = q_ref[0, 0, :, ls] * ATTN_SCALE
        k = jnp.concatenate([kp_ref[0, 0, :, ls], kc_ref[0, 0, :, ls]], axis=0)
        for hh in range(NH):
            s_ref[NH * t + hh] = _nt_dot(jnp.where(in_head[hh], q, jnp.zeros_like(q)), k)

    lse_tile = jnp.zeros((Q, LANES), F32)
    inv_den = []
    for h in range(DIL_HEADS):
        s = s_ref[h] + bias_ref[first, h]
        m = jnp.max(s, axis=-1, keepdims=True)
        e = jnp.exp(s - m)
        den = jnp.sum(e, axis=-1, keepdims=True)
        p_ref[h] = e.astype(BF16)
        inv_den.append(1.0 / den)
        lse_tile = jnp.where(lane == h, m + jnp.log(den), lse_tile)
    lse_ref[0] = lse_tile

    for t in range(n_tiles):
        ls = slice(t * LANES, (t + 1) * LANES)
        v = jnp.concatenate([vp_ref[0, 0, :, ls], vc_ref[0, 0, :, ls]], axis=0)
        o = [jnp.dot(p_ref[NH * t + hh], v, preferred_element_type=F32) * inv_den[NH * t + hh]
             for hh in range(NH)]
        o_ref[0, :, ls] = jnp.where(in_head[0], o[0], o[1])


def _dilated_group(qkv, col0, g, B, S):
    window, dilation = DIL_PAIRS[g]
    n = S // dilation
    nb = n // DIL_QBLK
    span = window // dilation

    def spec(piece, prev):
        if prev:
            return pl.BlockSpec((1, 1, DIL_QBLK, D_A),
                                lambda b, r, i: (b, r, jnp.maximum(i - 1, 0), col0 + piece))
        return pl.BlockSpec((1, 1, DIL_QBLK, D_A), lambda b, r, i: (b, r, i, col0 + piece))

    o, lse = pl.pallas_call(
        functools.partial(_dilated_kernel, dilation=dilation, span=span),
        grid=(B, dilation, nb),
        in_specs=[spec(0, False), spec(1, True), spec(1, False), spec(2, True), spec(2, False)],
        out_specs=[
            pl.BlockSpec((1, DIL_QBLK, D_A), lambda b, r, i: (b, i, r)),
            pl.BlockSpec((1, DIL_QBLK, LANES), lambda b, r, i: (b, i, r)),
        ],
        out_shape=[
            jax.ShapeDtypeStruct((B, n, dilation * D_A), F32),
            jax.ShapeDtypeStruct((B, n, dilation * LANES), F32),
        ],
        scratch_shapes=[
            pltpu.VMEM((2, DIL_HEADS, DIL_QBLK, 2 * DIL_QBLK), F32),
            pltpu.VMEM((DIL_HEADS, DIL_QBLK, 2 * DIL_QBLK), F32),
            pltpu.VMEM((DIL_HEADS, DIL_QBLK, 2 * DIL_QBLK), BF16),
        ],
        compiler_params=pltpu.CompilerParams(
            dimension_semantics=("arbitrary", "arbitrary", "arbitrary"), vmem_limit_bytes=VMEM_LIMIT),
        name=f"dilated_g{g}",
    )(qkv, qkv, qkv, qkv, qkv)
    return o.reshape(B, S, D_A), lse.reshape(B, S, LANES)


MOBA_PAIR = 2
MOBA_VROWS = HEAD_DIM + 16
SLOPES_B_LOG2 = tuple(s * LOG2E for s in SLOPES_B)


def _moba_kernel(q_ref, k_ref, v_ref, o_ref, kmh_ref, kmm_ref, kml_ref, k2_ref, vt_ref, q2t_ref,
                 sel_ref, m_ref, acc_ref, ra_ref, rb_ref, pa_ref, pb_ref, aa_ref, ab_ref, *, nblk):
    hp = pl.program_id(1)
    n = pl.program_id(2)
    BLK = MOBA_BLOCK
    NH = HEADS_PER_LANE_TILE
    PAIR = MOBA_PAIR
    ALIBI_LANE = (HEAD_DIM, 0)

    lane = lax.broadcasted_iota(jnp.int32, (BLK, LANES), 1)
    slopes, in_head, alibi_sel = [], [], []
    for hh in range(NH):
        in_head.append((lane >= hh * HEAD_DIM) & (lane < (hh + 1) * HEAD_DIM))
        alibi_sel.append([lane == ALIBI_LANE[hh] + t for t in range(3)])
        slope = jnp.float32(SLOPES_B_LOG2[hh])
        for p in range(1, MOBA_HEADS // NH):
            slope = jnp.where(hp == p, jnp.float32(SLOPES_B_LOG2[NH * p + hh]), slope)
        slopes.append(slope)

    @pl.when(n == 0)
    def _():
        key_off = lax.broadcasted_iota(jnp.int32, (BLK, LANES), 0).astype(F32)
        ones_tile = jnp.where(lax.broadcasted_iota(jnp.int32, (16, BLK), 0) == 0, 1.0, 0.0)
        cbs = []
        for hh in range(NH):
            terms = _split3(slopes[hh] * key_off)
            cb = jnp.zeros((BLK, LANES), F32)
            for t in range(3):
                cb = jnp.where(alibi_sel[hh][t], terms[t].astype(F32), cb)
            cbs.append(cb)

        def body(j, c):
            rows = pl.ds(pl.multiple_of(j * BLK, BLK), BLK)
            kb = k_ref[0, rows, :].astype(F32)
            km = jnp.sum(kb, axis=0, keepdims=True) * (1.0 / BLK)
            hi, mid, lo = _split3(km)
            kmh_ref[pl.ds(j, 1), :] = hi.astype(F32)
            kmm_ref[pl.ds(j, 1), :] = mid.astype(F32)
            kml_ref[pl.ds(j, 1), :] = lo.astype(F32)
            vt = v_ref[0, rows, :].astype(F32).T
            for hh in range(NH):
                k2_ref[j, hh] = jnp.where(in_head[hh], kb, cbs[hh]).astype(BF16)
                vh = vt[hh * HEAD_DIM:(hh + 1) * HEAD_DIM, :]
                vt_ref[j, hh] = jnp.concatenate([vh, ones_tile], axis=0).astype(BF16)
            return c
        lax.fori_loop(0, nblk, body, 0)
        rb_ref[...] = jnp.zeros((PAIR, NH, BLK, BLK), F32)
        pa_ref[...] = jnp.zeros((PAIR, NH, BLK, BLK), BF16)
        aa_ref[...] = jnp.ones((PAIR, NH, 1, BLK), F32)

    brow = lax.broadcasted_iota(jnp.int32, (nblk, NH * BLK), 0)
    brow_f = brow.astype(F32)
    kk = lax.broadcasted_iota(jnp.int32, (BLK, BLK), 0)
    tt = lax.broadcasted_iota(jnp.int32, (BLK, BLK), 1)

    q_all = q_ref[0]
    kmh = kmh_ref[...].astype(BF16)
    kmm = kmm_ref[...].astype(BF16)
    kml = kml_ref[...].astype(BF16)

    gates = []
    for hh in range(NH):
        qm = jnp.where(in_head[hh], q_all, jnp.zeros_like(q_all))
        is_alibi = alibi_sel[hh][0] | alibi_sel[hh][1] | alibi_sel[hh][2]
        q2t_ref[hh] = jnp.where(is_alibi, 1.0, qm.astype(F32)).T.astype(BF16)
        gates.append(_nt_dot(kmh, qm) + _nt_dot(kmm, qm) + _nt_dot(kml, qm))
    g = jnp.where(brow < n, jnp.concatenate(gates, axis=1), NEG_INF)
    for t in range(MOBA_TOPK):
        mx = jnp.max(g, axis=0, keepdims=True)
        idx = jnp.min(jnp.where(g == mx, brow_f, float(nblk)), axis=0, keepdims=True)
        sel_ref[pl.ds(t, 1), :] = jnp.where(t < n, idx, -1.0)
        g = jnp.where(brow_f == idx, -jnp.inf, g)

    for hh in range(NH):
        ra_ref[0, hh] = jnp.dot(k2_ref[n, hh], q2t_ref[hh], preferred_element_type=F32)
    for hh in range(NH):
        r = jnp.where(kk <= tt, ra_ref[0, hh], NEG_INF)
        m = jnp.max(r, axis=0, keepdims=True)
        pb_ref[0, hh] = jnp.exp2(r - m).astype(BF16)
        m_ref[hh] = m
    for hh in range(NH):
        acc_ref[hh] = jnp.dot(vt_ref[n, hh], pb_ref[0, hh], preferred_element_type=F32)

    last = jnp.maximum(n - 1, 0)

    def step(i, r_new, r_old, p_new, p_old, al_new, al_old):
        def scores():
            for k in range(PAIR):
                a = jnp.minimum(PAIR * i + k, last)
                for hh in range(NH):
                    r_new[k, hh] = jnp.dot(k2_ref[a, hh], q2t_ref[hh], preferred_element_type=F32)

        def pv():
            for hh in range(NH):
                acc = acc_ref[hh]
                for k in range(PAIR):
                    c = jnp.clip(PAIR * (i - 2) + k, 0, last)
                    acc = al_new[k, hh] * acc + jnp.dot(vt_ref[c, hh], p_new[k, hh],
                                                        preferred_element_type=F32)
                acc_ref[hh] = acc

        def softmax():
            for hh in range(NH):
                m = m_ref[hh]
                for k in range(PAIR):
                    b = PAIR * (i - 1) + k
                    bf = jnp.where((b >= 0) & (b < n), b, -2).astype(F32)
                    r = r_old[k, hh]
                    qs = slice(hh * BLK, (hh + 1) * BLK)
                    chosen = ((sel_ref[0:1, qs] == bf) | (sel_ref[1:2, qs] == bf)
                              | (sel_ref[2:3, qs] == bf))
                    off = jnp.where(chosen, -slopes[hh] * ((n - b) * BLK).astype(F32), NEG_INF)
                    m_new = jnp.maximum(m, jnp.max(r, axis=0, keepdims=True) + off)
                    al_old[k, hh] = jnp.exp2(m - m_new)
                    p_old[k, hh] = jnp.exp2(r - (m_new - off)).astype(BF16)
                    m = m_new
                m_ref[hh] = m

        for stage in (pv, softmax, scores):
            stage()

    def body(t, carry):
        step(2 * t, ra_ref, rb_ref, pa_ref, pb_ref, aa_ref, ab_ref)
        step(2 * t + 1, rb_ref, ra_ref, pb_ref, pa_ref, ab_ref, aa_ref)
        return carry

    n_steps = (n + PAIR - 1) // PAIR + 2
    lax.fori_loop(0, jnp.where(n > 0, (n_steps + 1) // 2, 0), body, 0)
    outs = []
    for hh in range(NH):
        a = acc_ref[hh]
        outs.append(a[:HEAD_DIM, :] / a[HEAD_DIM:HEAD_DIM + 1, :])
    o_ref[0] = jnp.concatenate(outs, axis=0).T


def _moba(proj, B, S):
    nblk = S // MOBA_BLOCK
    nh = HEADS_PER_LANE_TILE
    n_hp = MOBA_HEADS // nh
    cq, ck, cv = COL_BQ // LANES, COL_BK // LANES, COL_BV // LANES
    return pl.pallas_call(
        functools.partial(_moba_kernel, nblk=nblk),
        grid=(B, n_hp, nblk),
        in_specs=[
            pl.BlockSpec((1, MOBA_BLOCK, LANES), lambda b, h, n: (b, n, cq + h)),
            pl.BlockSpec((1, S, LANES), lambda b, h, n: (b, 0, ck + h), pipeline_mode=pl.Buffered(1)),
            pl.BlockSpec((1, S, LANES), lambda b, h, n: (b, 0, cv + h), pipeline_mode=pl.Buffered(1)),
        ],
        out_specs=pl.BlockSpec((1, MOBA_BLOCK, LANES), lambda b, h, n: (b, n, h)),
        out_shape=jax.ShapeDtypeStruct((B, S, D_B), F32),
        scratch_shapes=[pltpu.VMEM((nblk, LANES), F32)] * 3 + [
            pltpu.VMEM((nblk, nh, MOBA_BLOCK, LANES), BF16),
            pltpu.VMEM((nblk, nh, MOBA_VROWS, MOBA_BLOCK), BF16),
            pltpu.VMEM((nh, LANES, MOBA_BLOCK), BF16),
            pltpu.VMEM((8, nh * MOBA_BLOCK), F32),
            pltpu.VMEM((nh, 1, MOBA_BLOCK), F32),
            pltpu.VMEM((nh, MOBA_VROWS, MOBA_BLOCK), F32),
        ] + [pltpu.VMEM((MOBA_PAIR, nh, MOBA_BLOCK, MOBA_BLOCK), F32)] * 2
        + [pltpu.VMEM((MOBA_PAIR, nh, MOBA_BLOCK, MOBA_BLOCK), BF16)] * 2
        + [pltpu.VMEM((MOBA_PAIR, nh, 1, MOBA_BLOCK), F32)] * 2,
        compiler_params=pltpu.CompilerParams(
            dimension_semantics=("parallel", "parallel", "arbitrary"), vmem_limit_bytes=VMEM_LIMIT),
        name="moba",
    )(proj, proj, proj)


def _memkv_kernel(m_ref, nw_ref, w_ref, o_ref):
    h = _rmsnorm_bf16(m_ref[...], nw_ref[...])
    o_ref[...] = jnp.dot(h, w_ref[...], preferred_element_type=F32).astype(BF16)


def _memkv(mem2, mem_norm_w, w_bf16):
    R = mem2.shape[0]
    return pl.pallas_call(
        _memkv_kernel,
        out_shape=jax.ShapeDtypeStruct((R, 2 * D_M), BF16),
        compiler_params=pltpu.CompilerParams(vmem_limit_bytes=VMEM_LIMIT),
        name="memkv",
    )(mem2, mem_norm_w, w_bf16)


def _silu(x):
    return x * jax.nn.sigmoid(x)


def _merge_kernel(x_ref, o1_ref, o2_ref, o3_ref, l1_ref, l2_ref, l3_ref, ob_ref, ga_ref, gb_ref,
                  mq_ref, mg_ref, lg_ref, kv_ref, bm_ref, wa_ref, wb_ref, wm_ref, wo_ref, fw_ref,
                  out_ref):
    tm = x_ref.shape[1]
    l1, l2, l3 = l1_ref[0], l2_ref[0], l3_ref[0]
    lmax = jnp.maximum(jnp.maximum(l1, l2), l3)
    e1, e2, e3 = jnp.exp(l1 - lmax), jnp.exp(l2 - lmax), jnp.exp(l3 - lmax)
    inv = 1.0 / (e1 + e2 + e3)
    erow = lax.broadcasted_iota(jnp.int32, (LANES, D_A), 0)
    ecol = lax.broadcasted_iota(jnp.int32, (LANES, D_A), 1)
    expand = jnp.where((ecol >= erow * HEAD_DIM) & (ecol < (erow + 1) * HEAD_DIM), 1.0, 0.0).astype(BF16)

    def per_head(w):
        hi, mid, _ = _split3(w)
        return (jnp.dot(hi, expand, preferred_element_type=F32)
                + jnp.dot(mid, expand, preferred_element_type=F32))

    def merge_gate(k):
        cs = slice(k * D_MODEL, (k + 1) * D_MODEL)
        return jax.nn.sigmoid(lg_ref[0, :, cs].astype(F32) + bm_ref[:, cs])

    o_a = (per_head(e1 * inv) * o1_ref[0] + per_head(e2 * inv) * o2_ref[0]
           + per_head(e3 * inv) * o3_ref[0])
    t_a = (o_a * _silu(ga_ref[0].astype(F32))).astype(BF16)
    merged = merge_gate(0) * jnp.dot(t_a, wa_ref[...], preferred_element_type=F32)

    t_b = (ob_ref[0] * _silu(gb_ref[0].astype(F32))).astype(BF16)
    merged = merged + merge_gate(1) * jnp.dot(t_b, wb_ref[...], preferred_element_type=F32)

    mq = mq_ref[0] * ATTN_SCALE
    km = kv_ref[0, :, :D_M]
    vm = kv_ref[0, :, D_M:]
    mlane = lax.broadcasted_iota(jnp.int32, (tm, D_M), 1)
    o_m = jnp.zeros((tm, D_M), F32)
    for h in range(MEM_HEADS):
        in_head = (mlane >= h * HEAD_DIM) & (mlane < (h + 1) * HEAD_DIM)
        s = _nt_dot(jnp.where(in_head, mq, jnp.zeros_like(mq)), km)
        s = s - jnp.max(s, axis=-1, keepdims=True)
        e = jnp.exp(s)
        p = (e * (1.0 / jnp.sum(e, axis=-1, keepdims=True))).astype(BF16)
        o_m = jnp.where(in_head, jnp.dot(p, vm, preferred_element_type=F32), o_m)
    t_m = (o_m * _silu(mg_ref[0].astype(F32))).astype(BF16)
    merged = merged + merge_gate(2) * jnp.dot(t_m, wm_ref[...], preferred_element_type=F32)

    y = x_ref[0] + jnp.dot(merged.astype(BF16), wo_ref[...], preferred_element_type=F32)
    ms = jnp.mean(y * y, axis=-1, keepdims=True)
    out_ref[0] = y * lax.rsqrt(ms + EPS) * fw_ref[...]


def _merge(x, o_groups, lse_groups, o_b, proj, kv, b_merge, wa, wb, wm, wo, fw, tm=256):
    B, S, _ = x.shape

    def tok(width, colblk):
        return pl.BlockSpec((1, tm, width), lambda b, i: (b, i, colblk))

    def full(shape):
        return pl.BlockSpec(shape, lambda b, i: (0,) * len(shape))

    in_specs = (
        [tok(D_MODEL, 0)]
        + [tok(D_A, 0)] * 3
        + [tok(LANES, 0)] * 3
        + [tok(D_B, 0)]
        + [tok(D_A, COL_A_GATE // D_A), tok(D_B, COL_B_GATE // D_B),
           tok(D_M, COL_MQ // D_M), tok(D_M, COL_M_GATE // D_M), tok(D_MERGE, COL_MERGE // D_MERGE)]
        + [pl.BlockSpec((1, kv.shape[1], 2 * D_M), lambda b, i: (b, 0, 0))]
        + [full((1, D_MERGE)), full((D_A, D_MODEL)), full((D_B, D_MODEL)), full((D_M, D_MODEL)),
           full((D_MODEL, D_MODEL)), full((1, D_MODEL))]
    )
    return pl.pallas_call(
        _merge_kernel,
        grid=(B, S // tm),
        in_specs=in_specs,
        out_specs=tok(D_MODEL, 0),
        out_shape=jax.ShapeDtypeStruct((B, S, D_MODEL), F32),
        compiler_params=pltpu.CompilerParams(
            dimension_semantics=("parallel", "parallel"), vmem_limit_bytes=VMEM_LIMIT),
        name="merge",
    )(x, *o_groups, *lse_groups, o_b, proj, proj, proj, proj, proj, kv, b_merge, wa, wb, wm, wo, fw)


def kernel(x, mem, norm_w, mem_norm_w, w_in, b_merge, w_mem_kv, w_branch_a, w_branch_b,
           w_branch_m, w_out, final_norm_w):
    B, S, D = x.shape
    assert w_in.shape[0] == 1, "single-layer block"
    w = w_in[0]
    nw = norm_w[0][None, :]
    src_a_gate = N_GROUPS * D_QKV
    src_bq = src_a_gate + D_A
    src_bk = src_bq + D_B
    src_merge = src_bk + 3 * D_B + 2 * D_M
    w_bq = w[:, src_bq:src_bk] * (ATTN_SCALE * LOG2E)
    w_main = jnp.concatenate(
        [w[:, src_merge:], w[:, :D_QKV], w[:, src_a_gate:src_bq], w_bq, w[:, src_bk:src_merge]],
        axis=1).astype(BF16)
    proj = _inproj(x.reshape(B * S, D), nw, w_main, "inproj").reshape(B, S, D_MAIN)

    o_groups, lse_groups = [], []
    for g, (_, dilation) in enumerate(DIL_PAIRS):
        if dilation == 1:
            qkv, col0 = proj.reshape(B, 1, S, D_MAIN), COL_A0 // D_A
        else:
            x_g = x.reshape(B, S // dilation, dilation, D).transpose(0, 2, 1, 3).reshape(B * S, D)
            w_g = w[:, g * D_QKV:(g + 1) * D_QKV].astype(BF16)
            qkv = _inproj(x_g, nw, w_g, f"inproj_d{dilation}").reshape(B, dilation, S // dilation, D_QKV)
            col0 = 0
        o_g, lse_g = _dilated_group(qkv, col0, g, B, S)
        o_groups.append(o_g)
        lse_groups.append(lse_g)
    o_b = _moba(proj, B, S)
    kv = _memkv(mem.reshape(-1, D), mem_norm_w[0][None, :], w_mem_kv[0].astype(BF16))
    kv = kv.reshape(B, mem.shape[1], 2 * D_M)
    return _merge(x, o_groups, lse_groups, o_b, proj, kv, b_merge[0][None, :],
                  w_branch_a[0].astype(BF16), w_branch_b[0].astype(BF16),
                  w_branch_m[0].astype(BF16), w_out[0].astype(BF16), final_norm_w[None, :])
```

```python
import functools
import math

import jax
import jax.numpy as jnp
from jax import lax
from jax.experimental import pallas as pl
from jax.experimental.pallas import tpu as pltpu

F32 = jnp.float32
BF16 = jnp.bfloat16

D_MODEL = 1024
HEAD_DIM = 64
ATTN_SCALE = HEAD_DIM ** -0.5
LOG2E = math.log2(math.e)
DIL_PAIRS = ((128, 1), (512, 4), (2048, 16))
N_GROUPS = len(DIL_PAIRS)
DIL_HEADS = 8
DIL_QBLK = 128
MOBA_HEADS = 8
MOBA_BLOCK = 256
MOBA_TOPK = 3
MEM_HEADS = 4
N_BRANCHES = 3
EPS = 1e-6
NEG_INF = -1e30

D_A = DIL_HEADS * HEAD_DIM
D_B = MOBA_HEADS * HEAD_DIM
D_M = MEM_HEADS * HEAD_DIM
D_MERGE = N_BRANCHES * D_MODEL
D_QKV = 3 * D_A

COL_MERGE = 0
COL_A0 = D_MERGE
COL_A_GATE = COL_A0 + D_QKV
COL_BQ = COL_A_GATE + D_A
COL_BK = COL_BQ + D_B
COL_BV = COL_BK + D_B
COL_B_GATE = COL_BV + D_B
COL_MQ = COL_B_GATE + D_B
COL_M_GATE = COL_MQ + D_M
D_MAIN = COL_M_GATE + D_M

LANES = 128
HEADS_PER_LANE_TILE = LANES // HEAD_DIM

SLOPES_A = tuple(2.0 ** (-8.0 * k / 16) for k in range(1, 17, 2))
SLOPES_B = tuple(2.0 ** (-8.0 * k / 16) for k in range(2, 17, 2))

VMEM_LIMIT = 48 * 1024 * 1024


def _nt_dot(a, b):
    return lax.dot_general(a, b, (((1,), (1,)), ((), ())), preferred_element_type=F32)


def _split3(v):
    hi = v.astype(BF16)
    r1 = v - hi.astype(F32)
    mid = r1.astype(BF16)
    lo = (r1 - mid.astype(F32)).astype(BF16)
    return hi, mid, lo


def _rmsnorm_bf16(xf, w):
    ms = jnp.mean(xf * xf, axis=-1, keepdims=True)
    return (xf * lax.rsqrt(ms + EPS) * w).astype(BF16)


def _inproj_kernel(x_ref, nw_ref, w_ref, o_ref, h_ref):
    @pl.when(pl.program_id(1) == 0)
    def _():
        h_ref[...] = _rmsnorm_bf16(x_ref[...], nw_ref[...])

    o_ref[...] = jnp.dot(h_ref[...], w_ref[...], preferred_element_type=F32).astype(BF16)


def _inproj(x2, norm_w, w_bf16, name, tm=1024, tn=1536):
    T = x2.shape[0]
    n_cols = w_bf16.shape[1]
    return pl.pallas_call(
        _inproj_kernel,
        grid=(T // tm, n_cols // tn),
        in_specs=[
            pl.BlockSpec((tm, D_MODEL), lambda i, j: (i, 0)),
            pl.BlockSpec((1, D_MODEL), lambda i, j: (0, 0)),
            pl.BlockSpec((D_MODEL, tn), lambda i, j: (0, j)),
        ],
        out_specs=pl.BlockSpec((tm, tn), lambda i, j: (i, j)),
        out_shape=jax.ShapeDtypeStruct((T, n_cols), BF16),
        scratch_shapes=[pltpu.VMEM((tm, D_MODEL), BF16)],
        compiler_params=pltpu.CompilerParams(
            dimension_semantics=("parallel", "arbitrary"), vmem_limit_bytes=VMEM_LIMIT),
        name=name,
    )(x2, norm_w, w_bf16)


def _dilated_kernel(q_ref, kp_ref, kc_ref, vp_ref, vc_ref, o_ref, lse_ref, bias_ref, s_ref, p_ref,
                    *, dilation, span):
    blk = pl.program_id(2)
    Q = DIL_QBLK
    NH = HEADS_PER_LANE_TILE
    n_tiles = DIL_HEADS // NH

    @pl.when((pl.program_id(0) == 0) & (pl.program_id(1) == 0) & (blk == 0))
    def _():
        row = lax.broadcasted_iota(jnp.int32, (Q, 2 * Q), 0)
        col = lax.broadcasted_iota(jnp.int32, (Q, 2 * Q), 1)
        steps = row + Q - col
        in_band = (steps >= 0) & (steps <= span)
        dist = (steps * dilation).astype(F32)
        for h in range(DIL_HEADS):
            bias = jnp.where(in_band, -SLOPES_A[h] * dist, NEG_INF)
            bias_ref[0, h] = bias
            bias_ref[1, h] = jnp.where(col >= Q, bias, NEG_INF)

    lane = lax.broadcasted_iota(jnp.int32, (Q, LANES), 1)
    in_head = [(lane >= hh * HEAD_DIM) & (lane < (hh + 1) * HEAD_DIM) for hh in range(NH)]
    first = (blk == 0).astype(jnp.int32)

    for t in range(n_tiles):
        ls = slice(t * LANES, (t + 1) * LANES)
        q = q_ref[0, 0, :, ls] * ATTN_SCALE
        k = jnp.concatenate([kp_ref[0, 0, :, ls], kc_ref[0, 0, :, ls]], axis=0)
        for hh in range(NH):
            s_ref[NH * t + hh] = _nt_dot(jnp.where(in_head[hh], q, jnp.zeros_like(q)), k)

    lse_tile = jnp.zeros((Q, LANES), F32)
    inv_den = []
    for h in range(DIL_HEADS):
        s = s_ref[h] + bias_ref[first, h]
        m = jnp.max(s, axis=-1, keepdims=True)
        e = jnp.exp(s - m)
        den = jnp.sum(e, axis=-1, keepdims=True)
        p_ref[h] = e.astype(BF16)
        inv_den.append(1.0 / den)
        lse_tile = jnp.where(lane == h, m + jnp.log(den), lse_tile)
    lse_ref[0] = lse_tile

    for t in range(n_tiles):
        ls = slice(t * LANES, (t + 1) * LANES)
        v = jnp.concatenate([vp_ref[0, 0, :, ls], vc_ref[0, 0, :, ls]], axis=0)
        o = [jnp.dot(p_ref[NH * t + hh], v, preferred_element_type=F32) * inv_den[NH * t + hh]
             for hh in range(NH)]
        o_ref[0, :, ls] = jnp.where(in_head[0], o[0], o[1])


def _dilated_group(qkv, col0, g, B, S):
    window, dilation = DIL_PAIRS[g]
    n = S // dilation
    nb = n // DIL_QBLK
    span = window // dilation

    def spec(piece, prev):
        if prev:
            return pl.BlockSpec((1, 1, DIL_QBLK, D_A),
                                lambda b, r, i: (b, r, jnp.maximum(i - 1, 0), col0 + piece))
        return pl.BlockSpec((1, 1, DIL_QBLK, D_A), lambda b, r, i: (b, r, i, col0 + piece))

    o, lse = pl.pallas_call(
        functools.partial(_dilated_kernel, dilation=dilation, span=span),
        grid=(B, dilation, nb),
        in_specs=[spec(0, False), spec(1, True), spec(1, False), spec(2, True), spec(2, False)],
        out_specs=[
            pl.BlockSpec((1, DIL_QBLK, D_A), lambda b, r, i: (b, i, r)),
            pl.BlockSpec((1, DIL_QBLK, LANES), lambda b, r, i: (b, i, r)),
        ],
        out_shape=[
            jax.ShapeDtypeStruct((B, n, dilation * D_A), F32),
            jax.ShapeDtypeStruct((B, n, dilation * LANES), F32),
        ],
        scratch_shapes=[
            pltpu.VMEM((2, DIL_HEADS, DIL_QBLK, 2 * DIL_QBLK), F32),
            pltpu.VMEM((DIL_HEADS, DIL_QBLK, 2 * DIL_QBLK), F32),
            pltpu.VMEM((DIL_HEADS, DIL_QBLK, 2 * DIL_QBLK), BF16),
        ],
        compiler_params=pltpu.CompilerParams(
            dimension_semantics=("arbitrary", "arbitrary", "arbitrary"), vmem_limit_bytes=VMEM_LIMIT),
        name=f"dilated_g{g}",
    )(qkv, qkv, qkv, qkv, qkv)
    return o.reshape(B, S, D_A), lse.reshape(B, S, LANES)


MOBA_PAIR = 1
MOBA_QBLOCKS = 2
MOBA_VROWS = HEAD_DIM + 16
SLOPES_B_LOG2 = tuple(s * LOG2E for s in SLOPES_B)


def _moba_kernel(q_ref, k_ref, v_ref, o_ref, kmh_ref, kmm_ref, kml_ref, k2_ref, vt_ref, q2t_ref,
                 sel_ref, m_ref, acc_ref, ra_ref, rb_ref, pa_ref, pb_ref, aa_ref, ab_ref, *, nblk):
    hp = pl.program_id(1)
    tile = pl.program_id(2)
    BLK = MOBA_BLOCK
    NH = HEADS_PER_LANE_TILE
    PAIR = MOBA_PAIR
    QB = MOBA_QBLOCKS
    QT = QB * BLK
    ALIBI_LANE = (HEAD_DIM, 0)
    n0 = tile * QB
    n_past = n0 + QB - 1

    def lane_masks(rows):
        lane = lax.broadcasted_iota(jnp.int32, (rows, LANES), 1)
        in_head = [(lane >= hh * HEAD_DIM) & (lane < (hh + 1) * HEAD_DIM) for hh in range(NH)]
        alibi = [[lane == ALIBI_LANE[hh] + t for t in range(3)] for hh in range(NH)]
        return in_head, alibi

    slopes = []
    for hh in range(NH):
        slope = jnp.float32(SLOPES_B_LOG2[hh])
        for p in range(1, MOBA_HEADS // NH):
            slope = jnp.where(hp == p, jnp.float32(SLOPES_B_LOG2[NH * p + hh]), slope)
        slopes.append(slope)

    @pl.when(tile == 0)
    def _():
        in_head, alibi = lane_masks(BLK)
        key_off = lax.broadcasted_iota(jnp.int32, (BLK, LANES), 0).astype(F32)
        ones_tile = jnp.where(lax.broadcasted_iota(jnp.int32, (16, BLK), 0) == 0, 1.0, 0.0)
        cbs = []
        for hh in range(NH):
            terms = _split3(slopes[hh] * key_off)
            cb = jnp.zeros((BLK, LANES), F32)
            for t in range(3):
                cb = jnp.where(alibi[hh][t], terms[t].astype(F32), cb)
            cbs.append(cb)

        def body(j, c):
            rows = pl.ds(pl.multiple_of(j * BLK, BLK), BLK)
            kb = k_ref[0, rows, :].astype(F32)
            km = jnp.sum(kb, axis=0, keepdims=True) * (1.0 / BLK)
            hi, mid, lo = _split3(km)
            kmh_ref[pl.ds(j, 1), :] = hi.astype(F32)
            kmm_ref[pl.ds(j, 1), :] = mid.astype(F32)
            kml_ref[pl.ds(j, 1), :] = lo.astype(F32)
            vt = v_ref[0, rows, :].astype(F32).T
            for hh in range(NH):
                k2_ref[j, hh] = jnp.where(in_head[hh], kb, cbs[hh]).astype(BF16)
                vh = vt[hh * HEAD_DIM:(hh + 1) * HEAD_DIM, :]
                vt_ref[j, hh] = jnp.concatenate([vh, ones_tile], axis=0).astype(BF16)
            return c
        lax.fori_loop(0, nblk, body, 0)
        rb_ref[...] = jnp.zeros((PAIR, NH, BLK, QT), F32)
        pa_ref[...] = jnp.zeros((PAIR, NH, BLK, QT), BF16)
        aa_ref[...] = jnp.ones((PAIR, NH, 1, QT), F32)

    in_head_q, alibi_q = lane_masks(QT)
    qpos = lax.broadcasted_iota(jnp.int32, (1, NH * QT), 1)
    n_q = n0 + (qpos % QT) // BLK
    n_q_f = n_q.astype(F32)
    brow = lax.broadcasted_iota(jnp.int32, (nblk, NH * QT), 0)
    brow_f = brow.astype(F32)

    q_all = q_ref[0]
    kmh = kmh_ref[...].astype(BF16)
    kmm = kmm_ref[...].astype(BF16)
    kml = kml_ref[...].astype(BF16)

    gates = []
    for hh in range(NH):
        qm = jnp.where(in_head_q[hh], q_all, jnp.zeros_like(q_all))
        is_alibi = alibi_q[hh][0] | alibi_q[hh][1] | alibi_q[hh][2]
        q2t_ref[hh] = jnp.where(is_alibi, 1.0, qm.astype(F32)).T.astype(BF16)
        gates.append(_nt_dot(kmh, qm) + _nt_dot(kmm, qm) + _nt_dot(kml, qm))
    g = jnp.where(brow < n_q, jnp.concatenate(gates, axis=1), NEG_INF)
    for t in range(MOBA_TOPK):
        mx = jnp.max(g, axis=0, keepdims=True)
        idx = jnp.min(jnp.where(g == mx, brow_f, float(nblk)), axis=0, keepdims=True)
        sel_ref[pl.ds(t, 1), :] = jnp.where(t < n_q, idx, -1.0)
        g = jnp.where(brow_f == idx, -jnp.inf, g)

    kk = lax.broadcasted_iota(jnp.int32, (BLK, QT), 0)
    tt = lax.broadcasted_iota(jnp.int32, (BLK, QT), 1) % BLK
    for hh in range(NH):
        for qb in range(QB):
            qs = slice(qb * BLK, (qb + 1) * BLK)
            ra_ref[0, hh, :, qs] = jnp.dot(k2_ref[n0 + qb, hh], q2t_ref[hh, :, qs],
                                           preferred_element_type=F32)
    for hh in range(NH):
        r = jnp.where(kk <= tt, ra_ref[0, hh], NEG_INF)
        m = jnp.max(r, axis=0, keepdims=True)
        pb_ref[0, hh] = jnp.exp2(r - m).astype(BF16)
        m_ref[hh] = m
    for hh in range(NH):
        for qb in range(QB):
            qs = slice(qb * BLK, (qb + 1) * BLK)
            acc_ref[hh, :, qs] = jnp.dot(vt_ref[n0 + qb, hh], pb_ref[0, hh, :, qs],
                                         preferred_element_type=F32)

    last = n_past - 1

    def step(i, r_new, r_old, p_new, p_old, al_new, al_old):
        def scores():
            for k in range(PAIR):
                a = jnp.minimum(PAIR * i + k, last)
                for hh in range(NH):
                    r_new[k, hh] = jnp.dot(k2_ref[a, hh], q2t_ref[hh], preferred_element_type=F32)

        def pv():
            for hh in range(NH):
                acc = acc_ref[hh]
                for k in range(PAIR):
                    c = jnp.clip(PAIR * (i - 2) + k, 0, last)
                    acc = al_new[k, hh] * acc + jnp.dot(vt_ref[c, hh], p_new[k, hh],
                                                        preferred_element_type=F32)
                acc_ref[hh] = acc

        def softmax():
            for hh in range(NH):
                qs = slice(hh * QT, (hh + 1) * QT)
                m = m_ref[hh]
                for k in range(PAIR):
                    b = PAIR * (i - 1) + k
                    bf = jnp.where((b >= 0) & (b < n_past), b, -2).astype(F32)
                    r = r_old[k, hh]
                    chosen = ((sel_ref[0:1, qs] == bf) | (sel_ref[1:2, qs] == bf)
                              | (sel_ref[2:3, qs] == bf))
                    dist = (n_q_f[:, qs] - b.astype(F32)) * float(BLK)
                    off = jnp.where(chosen, -slopes[hh] * dist, NEG_INF)
                    m_new = jnp.maximum(m, jnp.max(r, axis=0, keepdims=True) + off)
                    al_old[k, hh] = jnp.exp2(m - m_new)
                    p_old[k, hh] = jnp.exp2(r - (m_new - off)).astype(BF16)
                    m = m_new
                m_ref[hh] = m

        for stage in (pv, softmax, scores):
            stage()

    def body(t, carry):
        step(2 * t, ra_ref, rb_ref, pa_ref, pb_ref, aa_ref, ab_ref)
        step(2 * t + 1, rb_ref, ra_ref, pb_ref, pa_ref, ab_ref, aa_ref)
        return carry

    n_steps = (n_past + PAIR - 1) // PAIR + 2
    lax.fori_loop(0, (n_steps + 1) // 2, body, 0)
    outs = []
    for hh in range(NH):
        a = acc_ref[hh]
        outs.append(a[:HEAD_DIM, :] / a[HEAD_DIM:HEAD_DIM + 1, :])
    o_ref[0] = jnp.concatenate(outs, axis=0).T


def _moba(proj, B, S):
    nblk = S // MOBA_BLOCK
    nh = HEADS_PER_LANE_TILE
    n_hp = MOBA_HEADS // nh
    qt = MOBA_QBLOCKS * MOBA_BLOCK
    cq, ck, cv = COL_BQ // LANES, COL_BK // LANES, COL_BV // LANES
    return pl.pallas_call(
        functools.partial(_moba_kernel, nblk=nblk),
        grid=(B, n_hp, S // qt),
        in_specs=[
            pl.BlockSpec((1, qt, LANES), lambda b, h, n: (b, n, cq + h)),
            pl.BlockSpec((1, S, LANES), lambda b, h, n: (b, 0, ck + h), pipeline_mode=pl.Buffered(1)),
            pl.BlockSpec((1, S, LANES), lambda b, h, n: (b, 0, cv + h), pipeline_mode=pl.Buffered(1)),
        ],
        out_specs=pl.BlockSpec((1, qt, LANES), lambda b, h, n: (b, n, h)),
        out_shape=jax.ShapeDtypeStruct((B, S, D_B), F32),
        scratch_shapes=[pltpu.VMEM((nblk, LANES), F32)] * 3 + [
            pltpu.VMEM((nblk, nh, MOBA_BLOCK, LANES), BF16),
            pltpu.VMEM((nblk, nh, MOBA_VROWS, MOBA_BLOCK), BF16),
            pltpu.VMEM((nh, LANES, qt), BF16),
            pltpu.VMEM((8, nh * qt), F32),
            pltpu.VMEM((nh, 1, qt), F32),
            pltpu.VMEM((nh, MOBA_VROWS, qt), F32),
        ] + [pltpu.VMEM((MOBA_PAIR, nh, MOBA_BLOCK, qt), F32)] * 2
        + [pltpu.VMEM((MOBA_PAIR, nh, MOBA_BLOCK, qt), BF16)] * 2
        + [pltpu.VMEM((MOBA_PAIR, nh, 1, qt), F32)] * 2,
        compiler_params=pltpu.CompilerParams(
            dimension_semantics=("parallel", "parallel", "arbitrary"), vmem_limit_bytes=VMEM_LIMIT),
        name="moba",
    )(proj, proj, proj)


def _memkv_kernel(m_ref, nw_ref, w_ref, o_ref):
    h = _rmsnorm_bf16(m_ref[...], nw_ref[...])
    o_ref[...] = jnp.dot(h, w_ref[...], preferred_element_type=F32).astype(BF16)


def _memkv(mem2, mem_norm_w, w_bf16):
    R = mem2.shape[0]
    return pl.pallas_call(
        _memkv_kernel,
        out_shape=jax.ShapeDtypeStruct((R, 2 * D_M), BF16),
        compiler_params=pltpu.CompilerParams(vmem_limit_bytes=VMEM_LIMIT),
        name="memkv",
    )(mem2, mem_norm_w, w_bf16)


def _silu(x):
    return x * jax.nn.sigmoid(x)


def _merge_kernel(x_ref, o1_ref, o2_ref, o3_ref, l1_ref, l2_ref, l3_ref, ob_ref, ga_ref, gb_ref,
                  mq_ref, mg_ref, lg_ref, kv_ref, bm_ref, wa_ref, wb_ref, wm_ref, wo_ref, fw_ref,
                  out_ref):
    tm = x_ref.shape[1]
    l1, l2, l3 = l1_ref[0], l2_ref[0], l3_ref[0]
    lmax = jnp.maximum(jnp.maximum(l1, l2), l3)
    e1, e2, e3 = jnp.exp(l1 - lmax), jnp.exp(l2 - lmax), jnp.exp(l3 - lmax)
    inv = 1.0 / (e1 + e2 + e3)
    erow = lax.broadcasted_iota(jnp.int32, (LANES, D_A), 0)
    ecol = lax.broadcasted_iota(jnp.int32, (LANES, D_A), 1)
    expand = jnp.where((ecol >= erow * HEAD_DIM) & (ecol < (erow + 1) * HEAD_DIM), 1.0, 0.0).astype(BF16)

    def per_head(w):
        hi, mid, _ = _split3(w)
        return (jnp.dot(hi, expand, preferred_element_type=F32)
                + jnp.dot(mid, expand, preferred_element_type=F32))

    def merge_gate(k):
        cs = slice(k * D_MODEL, (k + 1) * D_MODEL)
        return jax.nn.sigmoid(lg_ref[0, :, cs].astype(F32) + bm_ref[:, cs])

    o_a = (per_head(e1 * inv) * o1_ref[0] + per_head(e2 * inv) * o2_ref[0]
           + per_head(e3 * inv) * o3_ref[0])
    t_a = (o_a * _silu(ga_ref[0].astype(F32))).astype(BF16)
    merged = merge_gate(0) * jnp.dot(t_a, wa_ref[...], preferred_element_type=F32)

    t_b = (ob_ref[0] * _silu(gb_ref[0].astype(F32))).astype(BF16)
    merged = merged + merge_gate(1) * jnp.dot(t_b, wb_ref[...], preferred_element_type=F32)

    mq = mq_ref[0] * ATTN_SCALE
    km = kv_ref[0, :, :D_M]
    vm = kv_ref[0, :, D_M:]
    mlane = lax.broadcasted_iota(jnp.int32, (tm, D_M), 1)
    o_m = jnp.zeros((tm, D_M), F32)
    for h in range(MEM_HEADS):
        in_head = (mlane >= h * HEAD_DIM) & (mlane < (h + 1) * HEAD_DIM)
        s = _nt_dot(jnp.where(in_head, mq, jnp.zeros_like(mq)), km)
        s = s - jnp.max(s, axis=-1, keepdims=True)
        e = jnp.exp(s)
        p = (e * (1.0 / jnp.sum(e, axis=-1, keepdims=True))).astype(BF16)
        o_m = jnp.where(in_head, jnp.dot(p, vm, preferred_element_type=F32), o_m)
    t_m = (o_m * _silu(mg_ref[0].astype(F32))).astype(BF16)
    merged = merged + merge_gate(2) * jnp.dot(t_m, wm_ref[...], preferred_element_type=F32)

    y = x_ref[0] + jnp.dot(merged.astype(BF16), wo_ref[...], preferred_element_type=F32)
    ms = jnp.mean(y * y, axis=-1, keepdims=True)
    out_ref[0] = y * lax.rsqrt(ms + EPS) * fw_ref[...]


def _merge(x, o_groups, lse_groups, o_b, proj, kv, b_merge, wa, wb, wm, wo, fw, tm=256):
    B, S, _ = x.shape

    def tok(width, colblk):
        return pl.BlockSpec((1, tm, width), lambda b, i: (b, i, colblk))

    def full(shape):
        return pl.BlockSpec(shape, lambda b, i: (0,) * len(shape))

    in_specs = (
        [tok(D_MODEL, 0)]
        + [tok(D_A, 0)] * 3
        + [tok(LANES, 0)] * 3
        + [tok(D_B, 0)]
        + [tok(D_A, COL_A_GATE // D_A), tok(D_B, COL_B_GATE // D_B),
           tok(D_M, COL_MQ // D_M), tok(D_M, COL_M_GATE // D_M), tok(D_MERGE, COL_MERGE // D_MERGE)]
        + [pl.BlockSpec((1, kv.shape[1], 2 * D_M), lambda b, i: (b, 0, 0))]
        + [full((1, D_MERGE)), full((D_A, D_MODEL)), full((D_B, D_MODEL)), full((D_M, D_MODEL)),
           full((D_MODEL, D_MODEL)), full((1, D_MODEL))]
    )
    return pl.pallas_call(
        _merge_kernel,
        grid=(B, S // tm),
        in_specs=in_specs,
        out_specs=tok(D_MODEL, 0),
        out_shape=jax.ShapeDtypeStruct((B, S, D_MODEL), F32),
        compiler_params=pltpu.CompilerParams(
            dimension_semantics=("parallel", "parallel"), vmem_limit_bytes=VMEM_LIMIT),
        name="merge",
    )(x, *o_groups, *lse_groups, o_b, proj, proj, proj, proj, proj, kv, b_merge, wa, wb, wm, wo, fw)


def kernel(x, mem, norm_w, mem_norm_w, w_in, b_merge, w_mem_kv, w_branch_a, w_branch_b,
           w_branch_m, w_out, final_norm_w):
    B, S, D = x.shape
    assert w_in.shape[0] == 1, "single-layer block"
    w = w_in[0]
    nw = norm_w[0][None, :]
    src_a_gate = N_GROUPS * D_QKV
    src_bq = src_a_gate + D_A
    src_bk = src_bq + D_B
    src_merge = src_bk + 3 * D_B + 2 * D_M
    w_bq = w[:, src_bq:src_bk] * (ATTN_SCALE * LOG2E)
    w_main = jnp.concatenate(
        [w[:, src_merge:], w[:, :D_QKV], w[:, src_a_gate:src_bq], w_bq, w[:, src_bk:src_merge]],
        axis=1).astype(BF16)
    proj = _inproj(x.reshape(B * S, D), nw, w_main, "inproj").reshape(B, S, D_MAIN)

    o_groups, lse_groups = [], []
    for g, (_, dilation) in enumerate(DIL_PAIRS):
        if dilation == 1:
            qkv, col0 = proj.reshape(B, 1, S, D_MAIN), COL_A0 // D_A
        else:
            x_g = x.reshape(B, S // dilation, dilation, D).transpose(0, 2, 1, 3).reshape(B * S, D)
            w_g = w[:, g * D_QKV:(g + 1) * D_QKV].astype(BF16)
            qkv = _inproj(x_g, nw, w_g, f"inproj_d{dilation}").reshape(B, dilation, S // dilation, D_QKV)
            col0 = 0
        o_g, lse_g = _dilated_group(qkv, col0, g, B, S)
        o_groups.append(o_g)
        lse_groups.append(lse_g)
    o_b = _moba(proj, B, S)
    kv = _memkv(mem.reshape(-1, D), mem_norm_w[0][None, :], w_mem_kv[0].astype(BF16))
    kv = kv.reshape(B, mem.shape[1], 2 * D_M)
    return _merge(x, o_groups, lse_groups, o_b, proj, kv, b_merge[0][None, :],
                  w_branch_a[0].astype(BF16), w_branch_b[0].astype(BF16),
                  w_branch_m[0].astype(BF16), w_out[0].astype(BF16), final_norm_w[None, :])
```

```python
import functools
import math

import jax
import jax.numpy as jnp
from jax import lax
from jax.experimental import pallas as pl
from jax.experimental.pallas import tpu as pltpu

F32 = jnp.float32
BF16 = jnp.bfloat16

D_MODEL = 1024
HEAD_DIM = 64
ATTN_SCALE = HEAD_DIM ** -0.5
LOG2E = math.log2(math.e)
DIL_PAIRS = ((128, 1), (512, 4), (2048, 16))
N_GROUPS = len(DIL_PAIRS)
DIL_HEADS = 8
DIL_QBLK = 128
DIL_QBLOCKS = 2
MOBA_HEADS = 8
MOBA_BLOCK = 256
MOBA_TOPK = 3
MEM_HEADS = 4
N_BRANCHES = 3
EPS = 1e-6
NEG_INF = -1e30

D_A = DIL_HEADS * HEAD_DIM
D_B = MOBA_HEADS * HEAD_DIM
D_M = MEM_HEADS * HEAD_DIM
D_MERGE = N_BRANCHES * D_MODEL
D_QKV = 3 * D_A

COL_MERGE = 0
COL_A0 = D_MERGE
COL_A_GATE = COL_A0 + D_QKV
COL_BQ = COL_A_GATE + D_A
COL_BK = COL_BQ + D_B
COL_BV = COL_BK + D_B
COL_B_GATE = COL_BV + D_B
COL_MQ = COL_B_GATE + D_B
COL_M_GATE = COL_MQ + D_M
D_MAIN = COL_M_GATE + D_M

LANES = 128
HEADS_PER_LANE_TILE = LANES // HEAD_DIM

SLOPES_A = tuple(2.0 ** (-8.0 * k / 16) for k in range(1, 17, 2))
SLOPES_B = tuple(2.0 ** (-8.0 * k / 16) for k in range(2, 17, 2))

VMEM_LIMIT = 48 * 1024 * 1024


def _nt_dot(a, b):
    return lax.dot_general(a, b, (((1,), (1,)), ((), ())), preferred_element_type=F32)


def _split3(v):
    hi = v.astype(BF16)
    r1 = v - hi.astype(F32)
    mid = r1.astype(BF16)
    lo = (r1 - mid.astype(F32)).astype(BF16)
    return hi, mid, lo


def _rmsnorm_bf16(xf, w):
    ms = jnp.mean(xf * xf, axis=-1, keepdims=True)
    return (xf * lax.rsqrt(ms + EPS) * w).astype(BF16)


def _inproj_kernel(x_ref, nw_ref, w_ref, o_ref, h_ref):
    @pl.when(pl.program_id(1) == 0)
    def _():
        h_ref[...] = _rmsnorm_bf16(x_ref[...], nw_ref[...])

    o_ref[...] = jnp.dot(h_ref[...], w_ref[...], preferred_element_type=F32).astype(BF16)


def _inproj(x2, norm_w, w_bf16, name, tm=1024, tn=1536):
    T = x2.shape[0]
    n_cols = w_bf16.shape[1]
    return pl.pallas_call(
        _inproj_kernel,
        grid=(T // tm, n_cols // tn),
        in_specs=[
            pl.BlockSpec((tm, D_MODEL), lambda i, j: (i, 0)),
            pl.BlockSpec((1, D_MODEL), lambda i, j: (0, 0)),
            pl.BlockSpec((D_MODEL, tn), lambda i, j: (0, j)),
        ],
        out_specs=pl.BlockSpec((tm, tn), lambda i, j: (i, j)),
        out_shape=jax.ShapeDtypeStruct((T, n_cols), BF16),
        scratch_shapes=[pltpu.VMEM((tm, D_MODEL), BF16)],
        compiler_params=pltpu.CompilerParams(
            dimension_semantics=("parallel", "arbitrary"), vmem_limit_bytes=VMEM_LIMIT),
        name=name,
    )(x2, norm_w, w_bf16)


def _dilated_kernel(q_ref, kp_ref, kc_ref, vp_ref, vc_ref, o_ref, lse_ref, bias_ref, s_ref, p_ref,
                    *, dilation, span):
    blk = pl.program_id(2)
    Q = DIL_QBLK
    NH = HEADS_PER_LANE_TILE
    n_tiles = DIL_HEADS // NH

    @pl.when((pl.program_id(0) == 0) & (pl.program_id(1) == 0) & (blk == 0))
    def _():
        row = lax.broadcasted_iota(jnp.int32, (Q, 2 * Q), 0)
        col = lax.broadcasted_iota(jnp.int32, (Q, 2 * Q), 1)
        steps = row + Q - col
        in_band = (steps >= 0) & (steps <= span)
        dist = (steps * dilation).astype(F32)
        for h in range(DIL_HEADS):
            bias = jnp.where(in_band, -SLOPES_A[h] * dist, NEG_INF)
            bias_ref[0, h] = bias
            bias_ref[1, h] = jnp.where(col >= Q, bias, NEG_INF)

    lane = lax.broadcasted_iota(jnp.int32, (Q, LANES), 1)
    in_head = [(lane >= hh * HEAD_DIM) & (lane < (hh + 1) * HEAD_DIM) for hh in range(NH)]

    for qb in range(DIL_QBLOCKS):
        rows = slice(qb * Q, (qb + 1) * Q)
        variant = (blk == 0).astype(jnp.int32) if qb == 0 else 0

        def band(prev_ref, cur_ref, ls):
            if qb == 0:
                return jnp.concatenate([prev_ref[0, 0, :, ls], cur_ref[0, 0, :Q, ls]], axis=0)
            return cur_ref[0, 0, (qb - 1) * Q:(qb + 1) * Q, ls]

        for t in range(n_tiles):
            ls = slice(t * LANES, (t + 1) * LANES)
            q = q_ref[0, 0, rows, ls] * ATTN_SCALE
            k = band(kp_ref, kc_ref, ls)
            for hh in range(NH):
                s_ref[NH * t + hh] = _nt_dot(jnp.where(in_head[hh], q, jnp.zeros_like(q)), k)

        lse_tile = jnp.zeros((Q, LANES), F32)
        inv_den = []
        for h in range(DIL_HEADS):
            s = s_ref[h] + bias_ref[variant, h]
            m = jnp.max(s, axis=-1, keepdims=True)
            e = jnp.exp(s - m)
            den = jnp.sum(e, axis=-1, keepdims=True)
            p_ref[h] = e.astype(BF16)
            inv_den.append(1.0 / den)
            lse_tile = jnp.where(lane == h, m + jnp.log(den), lse_tile)
        lse_ref[0, rows, :] = lse_tile

        for t in range(n_tiles):
            ls = slice(t * LANES, (t + 1) * LANES)
            v = band(vp_ref, vc_ref, ls)
            o = [jnp.dot(p_ref[NH * t + hh], v, preferred_element_type=F32) * inv_den[NH * t + hh]
                 for hh in range(NH)]
            o_ref[0, rows, ls] = jnp.where(in_head[0], o[0], o[1]).astype(BF16)


def _dilated_group(qkv, col0, g, B, S):
    window, dilation = DIL_PAIRS[g]
    n = S // dilation
    nb = n // DIL_QBLK
    span = window // dilation

    rows = DIL_QBLOCKS * DIL_QBLK

    def spec(piece, prev):
        if prev:
            return pl.BlockSpec((1, 1, DIL_QBLK, D_A),
                                lambda b, r, i: (b, r, jnp.maximum(DIL_QBLOCKS * i - 1, 0), col0 + piece))
        return pl.BlockSpec((1, 1, rows, D_A), lambda b, r, i: (b, r, i, col0 + piece))

    o, lse = pl.pallas_call(
        functools.partial(_dilated_kernel, dilation=dilation, span=span),
        grid=(B, dilation, nb // DIL_QBLOCKS),
        in_specs=[spec(0, False), spec(1, True), spec(1, False), spec(2, True), spec(2, False)],
        out_specs=[
            pl.BlockSpec((1, rows, D_A), lambda b, r, i: (b, i, r)),
            pl.BlockSpec((1, rows, LANES), lambda b, r, i: (b, i, r)),
        ],
        out_shape=[
            jax.ShapeDtypeStruct((B, n, dilation * D_A), BF16),
            jax.ShapeDtypeStruct((B, n, dilation * LANES), F32),
        ],
        scratch_shapes=[
            pltpu.VMEM((2, DIL_HEADS, DIL_QBLK, 2 * DIL_QBLK), F32),
            pltpu.VMEM((DIL_HEADS, DIL_QBLK, 2 * DIL_QBLK), F32),
            pltpu.VMEM((DIL_HEADS, DIL_QBLK, 2 * DIL_QBLK), BF16),
        ],
        compiler_params=pltpu.CompilerParams(
            dimension_semantics=("arbitrary", "arbitrary", "arbitrary"), vmem_limit_bytes=VMEM_LIMIT),
        name=f"dilated_g{g}",
    )(qkv, qkv, qkv, qkv, qkv)
    return o.reshape(B, S, D_A), lse.reshape(B, S, LANES)


MOBA_PAIR = 1
MOBA_QBLOCKS = 2
MOBA_STEPS_PER_BODY = 4
MOBA_VROWS = HEAD_DIM + 16
SLOPES_B_LOG2 = tuple(s * LOG2E for s in SLOPES_B)


def _moba_kernel(q_ref, k_ref, v_ref, o_ref, kmh_ref, kmm_ref, kml_ref, k2_ref, vt_ref, q2t_ref,
                 sel_ref, m_ref, acc_ref, ra_ref, rb_ref, pa_ref, pb_ref, aa_ref, ab_ref, *, nblk):
    hp = pl.program_id(1)
    tile = pl.program_id(2)
    BLK = MOBA_BLOCK
    NH = HEADS_PER_LANE_TILE
    PAIR = MOBA_PAIR
    QB = MOBA_QBLOCKS
    QT = QB * BLK
    ALIBI_LANE = (HEAD_DIM, 0)
    n0 = tile * QB
    n_past = n0 + QB - 1

    def lane_masks(rows):
        lane = lax.broadcasted_iota(jnp.int32, (rows, LANES), 1)
        in_head = [(lane >= hh * HEAD_DIM) & (lane < (hh + 1) * HEAD_DIM) for hh in range(NH)]
        alibi = [[lane == ALIBI_LANE[hh] + t for t in range(3)] for hh in range(NH)]
        return in_head, alibi

    slopes = []
    for hh in range(NH):
        slope = jnp.float32(SLOPES_B_LOG2[hh])
        for p in range(1, MOBA_HEADS // NH):
            slope = jnp.where(hp == p, jnp.float32(SLOPES_B_LOG2[NH * p + hh]), slope)
        slopes.append(slope)

    @pl.when(tile == 0)
    def _():
        in_head, alibi = lane_masks(BLK)
        key_off = lax.broadcasted_iota(jnp.int32, (BLK, LANES), 0).astype(F32)
        ones_tile = jnp.where(lax.broadcasted_iota(jnp.int32, (16, BLK), 0) == 0, 1.0, 0.0)
        cbs = []
        for hh in range(NH):
            terms = _split3(slopes[hh] * key_off)
            cb = jnp.zeros((BLK, LANES), F32)
            for t in range(3):
                cb = jnp.where(alibi[hh][t], terms[t].astype(F32), cb)
            cbs.append(cb)

        def body(j, c):
            rows = pl.ds(pl.multiple_of(j * BLK, BLK), BLK)
            kb = k_ref[0, rows, :].astype(F32)
            km = jnp.sum(kb, axis=0, keepdims=True) * (1.0 / BLK)
            hi, mid, lo = _split3(km)
            kmh_ref[pl.ds(j, 1), :] = hi.astype(F32)
            kmm_ref[pl.ds(j, 1), :] = mid.astype(F32)
            kml_ref[pl.ds(j, 1), :] = lo.astype(F32)
            vt = v_ref[0, rows, :].astype(F32).T
            for hh in range(NH):
                k2_ref[j, hh] = jnp.where(in_head[hh], kb, cbs[hh]).astype(BF16)
                vh = vt[hh * HEAD_DIM:(hh + 1) * HEAD_DIM, :]
                vt_ref[j, hh] = jnp.concatenate([vh, ones_tile], axis=0).astype(BF16)
            return c
        lax.fori_loop(0, nblk, body, 0)
        rb_ref[...] = jnp.zeros((PAIR, NH, BLK, QT), F32)
        pa_ref[...] = jnp.zeros((PAIR, NH, BLK, QT), BF16)
        aa_ref[...] = jnp.ones((PAIR, NH, 1, QT), F32)

    in_head_q, alibi_q = lane_masks(QT)
    qpos = lax.broadcasted_iota(jnp.int32, (1, NH * QT), 1)
    n_q = n0 + (qpos % QT) // BLK
    n_q_f = n_q.astype(F32)
    brow = lax.broadcasted_iota(jnp.int32, (nblk, NH * QT), 0)
    brow_f = brow.astype(F32)

    q_all = q_ref[0]
    kmh = kmh_ref[...].astype(BF16)
    kmm = kmm_ref[...].astype(BF16)
    kml = kml_ref[...].astype(BF16)

    gates = []
    for hh in range(NH):
        qm = jnp.where(in_head_q[hh], q_all, jnp.zeros_like(q_all))
        is_alibi = alibi_q[hh][0] | alibi_q[hh][1] | alibi_q[hh][2]
        q2t_ref[hh] = jnp.where(is_alibi, 1.0, qm.astype(F32)).T.astype(BF16)
        gates.append(_nt_dot(kmh, qm) + _nt_dot(kmm, qm) + _nt_dot(kml, qm))
    g = jnp.where(brow < n_q, jnp.concatenate(gates, axis=1), NEG_INF)
    for t in range(MOBA_TOPK):
        mx = jnp.max(g, axis=0, keepdims=True)
        idx = jnp.min(jnp.where(g == mx, brow_f, float(nblk)), axis=0, keepdims=True)
        sel_ref[pl.ds(t, 1), :] = jnp.where(t < n_q, idx, -1.0)
        g = jnp.where(brow_f == idx, -jnp.inf, g)

    kk = lax.broadcasted_iota(jnp.int32, (BLK, QT), 0)
    tt = lax.broadcasted_iota(jnp.int32, (BLK, QT), 1) % BLK
    for hh in range(NH):
        for qb in range(QB):
            qs = slice(qb * BLK, (qb + 1) * BLK)
            ra_ref[0, hh, :, qs] = jnp.dot(k2_ref[n0 + qb, hh], q2t_ref[hh, :, qs],
                                           preferred_element_type=F32)
    for hh in range(NH):
        r = jnp.where(kk <= tt, ra_ref[0, hh], NEG_INF)
        m = jnp.max(r, axis=0, keepdims=True)
        pb_ref[0, hh] = jnp.exp2(r - m).astype(BF16)
        m_ref[hh] = m
    for hh in range(NH):
        for qb in range(QB):
            qs = slice(qb * BLK, (qb + 1) * BLK)
            acc_ref[hh, :, qs] = jnp.dot(vt_ref[n0 + qb, hh], pb_ref[0, hh, :, qs],
                                         preferred_element_type=F32)

    last = n_past - 1

    def step(i, r_new, r_old, p_new, p_old, al_new, al_old):
        def scores():
            for k in range(PAIR):
                a = jnp.minimum(PAIR * i + k, last)
                for hh in range(NH):
                    r_new[k, hh] = jnp.dot(k2_ref[a, hh], q2t_ref[hh], preferred_element_type=F32)

        def pv():
            for hh in range(NH):
                acc = acc_ref[hh]
                for k in range(PAIR):
                    c = jnp.clip(PAIR * (i - 2) + k, 0, last)
                    acc = al_new[k, hh] * acc + jnp.dot(vt_ref[c, hh], p_new[k, hh],
                                                        preferred_element_type=F32)
                acc_ref[hh] = acc

        def softmax():
            for hh in range(NH):
                qs = slice(hh * QT, (hh + 1) * QT)
                m = m_ref[hh]
                for k in range(PAIR):
                    b = PAIR * (i - 1) + k
                    bf = jnp.where((b >= 0) & (b < n_past), b, -2).astype(F32)
                    r = r_old[k, hh]
                    chosen = ((sel_ref[0:1, qs] == bf) | (sel_ref[1:2, qs] == bf)
                              | (sel_ref[2:3, qs] == bf))
                    dist = (n_q_f[:, qs] - b.astype(F32)) * float(BLK)
                    off = jnp.where(chosen, -slopes[hh] * dist, NEG_INF)
                    m_new = jnp.maximum(m, jnp.max(r, axis=0, keepdims=True) + off)
                    al_old[k, hh] = jnp.exp2(m - m_new)
                    p_old[k, hh] = jnp.exp2(r - (m_new - off)).astype(BF16)
                    m = m_new
                m_ref[hh] = m

        for stage in (pv, softmax, scores):
            stage()

    def body(t, carry):
        for u in range(0, MOBA_STEPS_PER_BODY, 2):
            step(MOBA_STEPS_PER_BODY * t + u, ra_ref, rb_ref, pa_ref, pb_ref, aa_ref, ab_ref)
            step(MOBA_STEPS_PER_BODY * t + u + 1, rb_ref, ra_ref, pb_ref, pa_ref, ab_ref, aa_ref)
        return carry

    n_steps = (n_past + PAIR - 1) // PAIR + 2
    lax.fori_loop(0, (n_steps + MOBA_STEPS_PER_BODY - 1) // MOBA_STEPS_PER_BODY, body, 0)
    outs = []
    for hh in range(NH):
        a = acc_ref[hh]
        outs.append(a[:HEAD_DIM, :] / a[HEAD_DIM:HEAD_DIM + 1, :])
    o_ref[0] = jnp.concatenate(outs, axis=0).T.astype(BF16)


def _moba(proj, B, S):
    nblk = S // MOBA_BLOCK
    nh = HEADS_PER_LANE_TILE
    n_hp = MOBA_HEADS // nh
    qt = MOBA_QBLOCKS * MOBA_BLOCK
    cq, ck, cv = COL_BQ // LANES, COL_BK // LANES, COL_BV // LANES
    return pl.pallas_call(
        functools.partial(_moba_kernel, nblk=nblk),
        grid=(B, n_hp, S // qt),
        in_specs=[
            pl.BlockSpec((1, qt, LANES), lambda b, h, n: (b, n, cq + h)),
            pl.BlockSpec((1, S, LANES), lambda b, h, n: (b, 0, ck + h), pipeline_mode=pl.Buffered(1)),
            pl.BlockSpec((1, S, LANES), lambda b, h, n: (b, 0, cv + h), pipeline_mode=pl.Buffered(1)),
        ],
        out_specs=pl.BlockSpec((1, qt, LANES), lambda b, h, n: (b, n, h)),
        out_shape=jax.ShapeDtypeStruct((B, S, D_B), BF16),
        scratch_shapes=[pltpu.VMEM((nblk, LANES), F32)] * 3 + [
            pltpu.VMEM((nblk, nh, MOBA_BLOCK, LANES), BF16),
            pltpu.VMEM((nblk, nh, MOBA_VROWS, MOBA_BLOCK), BF16),
            pltpu.VMEM((nh, LANES, qt), BF16),
            pltpu.VMEM((8, nh * qt), F32),
            pltpu.VMEM((nh, 1, qt), F32),
            pltpu.VMEM((nh, MOBA_VROWS, qt), F32),
        ] + [pltpu.VMEM((MOBA_PAIR, nh, MOBA_BLOCK, qt), F32)] * 2
        + [pltpu.VMEM((MOBA_PAIR, nh, MOBA_BLOCK, qt), BF16)] * 2
        + [pltpu.VMEM((MOBA_PAIR, nh, 1, qt), F32)] * 2,
        compiler_params=pltpu.CompilerParams(
            dimension_semantics=("parallel", "parallel", "arbitrary"), vmem_limit_bytes=VMEM_LIMIT),
        name="moba",
    )(proj, proj, proj)


def _memkv_kernel(m_ref, nw_ref, w_ref, o_ref):
    h = _rmsnorm_bf16(m_ref[...], nw_ref[...])
    o_ref[...] = jnp.dot(h, w_ref[...], preferred_element_type=F32).astype(BF16)


def _memkv(mem2, mem_norm_w, w_bf16):
    R = mem2.shape[0]
    return pl.pallas_call(
        _memkv_kernel,
        out_shape=jax.ShapeDtypeStruct((R, 2 * D_M), BF16),
        compiler_params=pltpu.CompilerParams(vmem_limit_bytes=VMEM_LIMIT),
        name="memkv",
    )(mem2, mem_norm_w, w_bf16)


def _silu(x):
    return x * jax.nn.sigmoid(x)


def _merge_kernel(x_ref, o1_ref, o2_ref, o3_ref, l1_ref, l2_ref, l3_ref, ob_ref, ga_ref, gb_ref,
                  mq_ref, mg_ref, lg_ref, kv_ref, bm_ref, wa_ref, wb_ref, wm_ref, wo_ref, fw_ref,
                  out_ref):
    tm = x_ref.shape[1]
    l1, l2, l3 = l1_ref[0], l2_ref[0], l3_ref[0]
    lmax = jnp.maximum(jnp.maximum(l1, l2), l3)
    e1, e2, e3 = jnp.exp(l1 - lmax), jnp.exp(l2 - lmax), jnp.exp(l3 - lmax)
    inv = 1.0 / (e1 + e2 + e3)
    erow = lax.broadcasted_iota(jnp.int32, (LANES, D_A), 0)
    ecol = lax.broadcasted_iota(jnp.int32, (LANES, D_A), 1)
    expand = jnp.where((ecol >= erow * HEAD_DIM) & (ecol < (erow + 1) * HEAD_DIM), 1.0, 0.0).astype(BF16)

    def per_head(w):
        hi, mid, _ = _split3(w)
        return (jnp.dot(hi, expand, preferred_element_type=F32)
                + jnp.dot(mid, expand, preferred_element_type=F32))

    def merge_gate(k):
        cs = slice(k * D_MODEL, (k + 1) * D_MODEL)
        return jax.nn.sigmoid(lg_ref[0, :, cs].astype(F32) + bm_ref[:, cs])

    o_a = (per_head(e1 * inv) * o1_ref[0].astype(F32) + per_head(e2 * inv) * o2_ref[0].astype(F32)
           + per_head(e3 * inv) * o3_ref[0].astype(F32))
    t_a = (o_a * _silu(ga_ref[0].astype(F32))).astype(BF16)
    merged = merge_gate(0) * jnp.dot(t_a, wa_ref[...], preferred_element_type=F32)

    t_b = (ob_ref[0].astype(F32) * _silu(gb_ref[0].astype(F32))).astype(BF16)
    merged = merged + merge_gate(1) * jnp.dot(t_b, wb_ref[...], preferred_element_type=F32)

    mq = mq_ref[0] * ATTN_SCALE
    km = kv_ref[0, :, :D_M]
    vm = kv_ref[0, :, D_M:]
    mlane = lax.broadcasted_iota(jnp.int32, (tm, D_M), 1)
    o_m = jnp.zeros((tm, D_M), F32)
    for h in range(MEM_HEADS):
        in_head = (mlane >= h * HEAD_DIM) & (mlane < (h + 1) * HEAD_DIM)
        s = _nt_dot(jnp.where(in_head, mq, jnp.zeros_like(mq)), km)
        s = s - jnp.max(s, axis=-1, keepdims=True)
        e = jnp.exp(s)
        p = (e * (1.0 / jnp.sum(e, axis=-1, keepdims=True))).astype(BF16)
        o_m = jnp.where(in_head, jnp.dot(p, vm, preferred_element_type=F32), o_m)
    t_m = (o_m * _silu(mg_ref[0].astype(F32))).astype(BF16)
    merged = merged + merge_gate(2) * jnp.dot(t_m, wm_ref[...], preferred_element_type=F32)

    y = x_ref[0] + jnp.dot(merged.astype(BF16), wo_ref[...], preferred_element_type=F32)
    ms = jnp.mean(y * y, axis=-1, keepdims=True)
    out_ref[0] = y * lax.rsqrt(ms + EPS) * fw_ref[...]


def _merge(x, o_groups, lse_groups, o_b, proj, kv, b_merge, wa, wb, wm, wo, fw, tm=256):
    B, S, _ = x.shape

    def tok(width, colblk):
        return pl.BlockSpec((1, tm, width), lambda b, i: (b, i, colblk))

    def full(shape):
        return pl.BlockSpec(shape, lambda b, i: (0,) * len(shape))

    in_specs = (
        [tok(D_MODEL, 0)]
        + [tok(D_A, 0)] * 3
        + [tok(LANES, 0)] * 3
        + [tok(D_B, 0)]
        + [tok(D_A, COL_A_GATE // D_A), tok(D_B, COL_B_GATE // D_B),
           tok(D_M, COL_MQ // D_M), tok(D_M, COL_M_GATE // D_M), tok(D_MERGE, COL_MERGE // D_MERGE)]
        + [pl.BlockSpec((1, kv.shape[1], 2 * D_M), lambda b, i: (b, 0, 0))]
        + [full((1, D_MERGE)), full((D_A, D_MODEL)), full((D_B, D_MODEL)), full((D_M, D_MODEL)),
           full((D_MODEL, D_MODEL)), full((1, D_MODEL))]
    )
    return pl.pallas_call(
        _merge_kernel,
        grid=(B, S // tm),
        in_specs=in_specs,
        out_specs=tok(D_MODEL, 0),
        out_shape=jax.ShapeDtypeStruct((B, S, D_MODEL), F32),
        compiler_params=pltpu.CompilerParams(
            dimension_semantics=("parallel", "parallel"), vmem_limit_bytes=VMEM_LIMIT),
        name="merge",
    )(x, *o_groups, *lse_groups, o_b, proj, proj, proj, proj, proj, kv, b_merge, wa, wb, wm, wo, fw)


def kernel(x, mem, norm_w, mem_norm_w, w_in, b_merge, w_mem_kv, w_branch_a, w_branch_b,
           w_branch_m, w_out, final_norm_w):
    B, S, D = x.shape
    assert w_in.shape[0] == 1, "single-layer block"
    w = w_in[0]
    nw = norm_w[0][None, :]
    src_a_gate = N_GROUPS * D_QKV
    src_bq = src_a_gate + D_A
    src_bk = src_bq + D_B
    src_merge = src_bk + 3 * D_B + 2 * D_M
    w_bq = w[:, src_bq:src_bk] * (ATTN_SCALE * LOG2E)
    w_main = jnp.concatenate(
        [w[:, src_merge:], w[:, :D_QKV], w[:, src_a_gate:src_bq], w_bq, w[:, src_bk:src_merge]],
        axis=1).astype(BF16)
    proj = _inproj(x.reshape(B * S, D), nw, w_main, "inproj").reshape(B, S, D_MAIN)

    o_groups, lse_groups = [], []
    for g, (_, dilation) in enumerate(DIL_PAIRS):
        if dilation == 1:
            qkv, col0 = proj.reshape(B, 1, S, D_MAIN), COL_A0 // D_A
        else:
            x_g = x.reshape(B, S // dilation, dilation, D).transpose(0, 2, 1, 3).reshape(B * S, D)
            w_g = w[:, g * D_QKV:(g + 1) * D_QKV].astype(BF16)
            qkv = _inproj(x_g, nw, w_g, f"inproj_d{dilation}").reshape(B, dilation, S // dilation, D_QKV)
            col0 = 0
        o_g, lse_g = _dilated_group(qkv, col0, g, B, S)
        o_groups.append(o_g)
        lse_groups.append(lse_g)
    o_b = _moba(proj, B, S)
    kv = _memkv(mem.reshape(-1, D), mem_norm_w[0][None, :], w_mem_kv[0].astype(BF16))
    kv = kv.reshape(B, mem.shape[1], 2 * D_M)
    return _merge(x, o_groups, lse_groups, o_b, proj, kv, b_merge[0][None, :],
                  w_branch_a[0].astype(BF16), w_branch_b[0].astype(BF16),
                  w_branch_m[0].astype(BF16), w_out[0].astype(BF16), final_norm_w[None, :])
```

```python
import functools
import math

import jax
import jax.numpy as jnp
from jax import lax
from jax.experimental import pallas as pl
from jax.experimental.pallas import tpu as pltpu

F32 = jnp.float32
BF16 = jnp.bfloat16

D_MODEL = 1024
HEAD_DIM = 64
ATTN_SCALE = HEAD_DIM ** -0.5
LOG2E = math.log2(math.e)
DIL_PAIRS = ((128, 1), (512, 4), (2048, 16))
N_GROUPS = len(DIL_PAIRS)
DIL_HEADS = 8
DIL_QBLK = 128
DIL_QBLOCKS = 4
MOBA_HEADS = 8
MOBA_BLOCK = 256
MOBA_TOPK = 3
MEM_HEADS = 4
N_BRANCHES = 3
EPS = 1e-6
NEG_INF = -1e30

D_A = DIL_HEADS * HEAD_DIM
D_B = MOBA_HEADS * HEAD_DIM
D_M = MEM_HEADS * HEAD_DIM
D_MERGE = N_BRANCHES * D_MODEL
D_QKV = 3 * D_A

COL_MERGE = 0
COL_A0 = D_MERGE
COL_A_GATE = COL_A0 + D_QKV
COL_BQ = COL_A_GATE + D_A
COL_BK = COL_BQ + D_B
COL_BV = COL_BK + D_B
COL_B_GATE = COL_BV + D_B
COL_MQ = COL_B_GATE + D_B
COL_M_GATE = COL_MQ + D_M
D_MAIN = COL_M_GATE + D_M

LANES = 128
HEADS_PER_LANE_TILE = LANES // HEAD_DIM

SLOPES_A = tuple(2.0 ** (-8.0 * k / 16) for k in range(1, 17, 2))
SLOPES_B = tuple(2.0 ** (-8.0 * k / 16) for k in range(2, 17, 2))

VMEM_LIMIT = 48 * 1024 * 1024


def _nt_dot(a, b):
    return lax.dot_general(a, b, (((1,), (1,)), ((), ())), preferred_element_type=F32)


def _split3(v):
    hi = v.astype(BF16)
    r1 = v - hi.astype(F32)
    mid = r1.astype(BF16)
    lo = (r1 - mid.astype(F32)).astype(BF16)
    return hi, mid, lo


def _rmsnorm_bf16(xf, w):
    ms = jnp.mean(xf * xf, axis=-1, keepdims=True)
    return (xf * lax.rsqrt(ms + EPS) * w).astype(BF16)


def _inproj_kernel(x_ref, nw_ref, w_ref, o_ref, h_ref):
    @pl.when(pl.program_id(1) == 0)
    def _():
        h_ref[...] = _rmsnorm_bf16(x_ref[...], nw_ref[...])

    o_ref[...] = jnp.dot(h_ref[...], w_ref[...], preferred_element_type=F32).astype(BF16)


def _inproj(x2, norm_w, w_bf16, name, tm=1024, tn=1536):
    T = x2.shape[0]
    n_cols = w_bf16.shape[1]
    return pl.pallas_call(
        _inproj_kernel,
        grid=(T // tm, n_cols // tn),
        in_specs=[
            pl.BlockSpec((tm, D_MODEL), lambda i, j: (i, 0)),
            pl.BlockSpec((1, D_MODEL), lambda i, j: (0, 0)),
            pl.BlockSpec((D_MODEL, tn), lambda i, j: (0, j)),
        ],
        out_specs=pl.BlockSpec((tm, tn), lambda i, j: (i, j)),
        out_shape=jax.ShapeDtypeStruct((T, n_cols), BF16),
        scratch_shapes=[pltpu.VMEM((tm, D_MODEL), BF16)],
        compiler_params=pltpu.CompilerParams(
            dimension_semantics=("parallel", "arbitrary"), vmem_limit_bytes=VMEM_LIMIT),
        name=name,
    )(x2, norm_w, w_bf16)


def _dilated_kernel(q_ref, kp_ref, kc_ref, vp_ref, vc_ref, o_ref, lse_ref, bias_ref, s_ref, p_ref,
                    *, dilation, span):
    blk = pl.program_id(2)
    Q = DIL_QBLK
    NH = HEADS_PER_LANE_TILE
    n_tiles = DIL_HEADS // NH

    @pl.when((pl.program_id(0) == 0) & (pl.program_id(1) == 0) & (blk == 0))
    def _():
        row = lax.broadcasted_iota(jnp.int32, (Q, 2 * Q), 0)
        col = lax.broadcasted_iota(jnp.int32, (Q, 2 * Q), 1)
        steps = row + Q - col
        in_band = (steps >= 0) & (steps <= span)
        dist = (steps * dilation).astype(F32)
        for h in range(DIL_HEADS):
            bias = jnp.where(in_band, -SLOPES_A[h] * dist, NEG_INF)
            bias_ref[0, h] = bias
            bias_ref[1, h] = jnp.where(col >= Q, bias, NEG_INF)

    lane = lax.broadcasted_iota(jnp.int32, (Q, LANES), 1)
    in_head = [(lane >= hh * HEAD_DIM) & (lane < (hh + 1) * HEAD_DIM) for hh in range(NH)]

    for qb in range(DIL_QBLOCKS):
        rows = slice(qb * Q, (qb + 1) * Q)
        variant = (blk == 0).astype(jnp.int32) if qb == 0 else 0

        def band(prev_ref, cur_ref, ls):
            if qb == 0:
                return jnp.concatenate([prev_ref[0, 0, :, ls], cur_ref[0, 0, :Q, ls]], axis=0)
            return cur_ref[0, 0, (qb - 1) * Q:(qb + 1) * Q, ls]

        for t in range(n_tiles):
            ls = slice(t * LANES, (t + 1) * LANES)
            q = q_ref[0, 0, rows, ls] * ATTN_SCALE
            k = band(kp_ref, kc_ref, ls)
            for hh in range(NH):
                s_ref[NH * t + hh] = _nt_dot(jnp.where(in_head[hh], q, jnp.zeros_like(q)), k)

        lse_tile = jnp.zeros((Q, LANES), F32)
        inv_den = []
        for h in range(DIL_HEADS):
            s = s_ref[h] + bias_ref[variant, h]
            m = jnp.max(s, axis=-1, keepdims=True)
            e = jnp.exp(s - m)
            den = jnp.sum(e, axis=-1, keepdims=True)
            p_ref[h] = e.astype(BF16)
            inv_den.append(1.0 / den)
            lse_tile = jnp.where(lane == h, m + jnp.log(den), lse_tile)
        lse_ref[0, rows, :] = lse_tile

        for t in range(n_tiles):
            ls = slice(t * LANES, (t + 1) * LANES)
            v = band(vp_ref, vc_ref, ls)
            o = [jnp.dot(p_ref[NH * t + hh], v, preferred_element_type=F32) * inv_den[NH * t + hh]
                 for hh in range(NH)]
            o_ref[0, rows, ls] = jnp.where(in_head[0], o[0], o[1]).astype(BF16)


def _dilated_group(qkv, col0, g, B, S):
    window, dilation = DIL_PAIRS[g]
    n = S // dilation
    nb = n // DIL_QBLK
    span = window // dilation

    rows = DIL_QBLOCKS * DIL_QBLK

    def spec(piece, prev):
        if prev:
            return pl.BlockSpec((1, 1, DIL_QBLK, D_A),
                                lambda b, r, i: (b, r, jnp.maximum(DIL_QBLOCKS * i - 1, 0), col0 + piece))
        return pl.BlockSpec((1, 1, rows, D_A), lambda b, r, i: (b, r, i, col0 + piece))

    o, lse = pl.pallas_call(
        functools.partial(_dilated_kernel, dilation=dilation, span=span),
        grid=(B, dilation, nb // DIL_QBLOCKS),
        in_specs=[spec(0, False), spec(1, True), spec(1, False), spec(2, True), spec(2, False)],
        out_specs=[
            pl.BlockSpec((1, rows, D_A), lambda b, r, i: (b, i, r)),
            pl.BlockSpec((1, rows, LANES), lambda b, r, i: (b, i, r)),
        ],
        out_shape=[
            jax.ShapeDtypeStruct((B, n, dilation * D_A), BF16),
            jax.ShapeDtypeStruct((B, n, dilation * LANES), F32),
        ],
        scratch_shapes=[
            pltpu.VMEM((2, DIL_HEADS, DIL_QBLK, 2 * DIL_QBLK), F32),
            pltpu.VMEM((DIL_HEADS, DIL_QBLK, 2 * DIL_QBLK), F32),
            pltpu.VMEM((DIL_HEADS, DIL_QBLK, 2 * DIL_QBLK), BF16),
        ],
        compiler_params=pltpu.CompilerParams(
            dimension_semantics=("arbitrary", "arbitrary", "arbitrary"), vmem_limit_bytes=VMEM_LIMIT),
        name=f"dilated_g{g}",
    )(qkv, qkv, qkv, qkv, qkv)
    return o.reshape(B, S, D_A), lse.reshape(B, S, LANES)


MOBA_PAIR = 1
MOBA_QBLOCKS = 2
MOBA_STEPS_PER_BODY = 4
MOBA_VROWS = HEAD_DIM + 16
SLOPES_B_LOG2 = tuple(s * LOG2E for s in SLOPES_B)


def _moba_kernel(q_ref, k_ref, v_ref, o_ref, kmh_ref, kmm_ref, kml_ref, k2_ref, vt_ref, q2t_ref,
                 sel_ref, m_ref, acc_ref, ra_ref, rb_ref, pa_ref, pb_ref, aa_ref, ab_ref, *, nblk):
    hp = pl.program_id(1)
    tile = pl.program_id(2)
    BLK = MOBA_BLOCK
    NH = HEADS_PER_LANE_TILE
    PAIR = MOBA_PAIR
    QB = MOBA_QBLOCKS
    QT = QB * BLK
    ALIBI_LANE = (HEAD_DIM, 0)
    n0 = tile * QB
    n_past = n0 + QB - 1

    def lane_masks(rows):
        lane = lax.broadcasted_iota(jnp.int32, (rows, LANES), 1)
        in_head = [(lane >= hh * HEAD_DIM) & (lane < (hh + 1) * HEAD_DIM) for hh in range(NH)]
        alibi = [[lane == ALIBI_LANE[hh] + t for t in range(3)] for hh in range(NH)]
        return in_head, alibi

    slopes = []
    for hh in range(NH):
        slope = jnp.float32(SLOPES_B_LOG2[hh])
        for p in range(1, MOBA_HEADS // NH):
            slope = jnp.where(hp == p, jnp.float32(SLOPES_B_LOG2[NH * p + hh]), slope)
        slopes.append(slope)

    @pl.when(tile == 0)
    def _():
        in_head, alibi = lane_masks(BLK)
        key_off = lax.broadcasted_iota(jnp.int32, (BLK, LANES), 0).astype(F32)
        ones_tile = jnp.where(lax.broadcasted_iota(jnp.int32, (16, BLK), 0) == 0, 1.0, 0.0)
        cbs = []
        for hh in range(NH):
            terms = _split3(slopes[hh] * key_off)
            cb = jnp.zeros((BLK, LANES), F32)
            for t in range(3):
                cb = jnp.where(alibi[hh][t], terms[t].astype(F32), cb)
            cbs.append(cb)

        def body(j, c):
            rows = pl.ds(pl.multiple_of(j * BLK, BLK), BLK)
            kb = k_ref[0, rows, :].astype(F32)
            km = jnp.sum(kb, axis=0, keepdims=True) * (1.0 / BLK)
            hi, mid, lo = _split3(km)
            kmh_ref[pl.ds(j, 1), :] = hi.astype(F32)
            kmm_ref[pl.ds(j, 1), :] = mid.astype(F32)
            kml_ref[pl.ds(j, 1), :] = lo.astype(F32)
            vt = v_ref[0, rows, :].astype(F32).T
            for hh in range(NH):
                k2_ref[j, hh] = jnp.where(in_head[hh], kb, cbs[hh]).astype(BF16)
                vh = vt[hh * HEAD_DIM:(hh + 1) * HEAD_DIM, :]
                vt_ref[j, hh] = jnp.concatenate([vh, ones_tile], axis=0).astype(BF16)
            return c
        lax.fori_loop(0, nblk, body, 0)
        rb_ref[...] = jnp.zeros((PAIR, NH, BLK, QT), F32)
        pa_ref[...] = jnp.zeros((PAIR, NH, BLK, QT), BF16)
        aa_ref[...] = jnp.ones((PAIR, NH, 1, QT), F32)

    in_head_q, alibi_q = lane_masks(QT)
    qpos = lax.broadcasted_iota(jnp.int32, (1, NH * QT), 1)
    n_q = n0 + (qpos % QT) // BLK
    n_q_f = n_q.astype(F32)
    brow = lax.broadcasted_iota(jnp.int32, (nblk, NH * QT), 0)
    brow_f = brow.astype(F32)

    q_all = q_ref[0]
    kmh = kmh_ref[...].astype(BF16)
    kmm = kmm_ref[...].astype(BF16)
    kml = kml_ref[...].astype(BF16)

    qms = []
    for hh in range(NH):
        qm = jnp.where(in_head_q[hh], q_all, jnp.zeros_like(q_all))
        is_alibi = alibi_q[hh][0] | alibi_q[hh][1] | alibi_q[hh][2]
        q2t_ref[hh] = jnp.where(is_alibi, 1.0, qm.astype(F32)).T.astype(BF16)
        qms.append(qm)
    for hh in range(NH):
        for qb in range(QB):
            qs = slice(qb * BLK, (qb + 1) * BLK)
            ra_ref[0, hh, :, qs] = jnp.dot(k2_ref[n0 + qb, hh], q2t_ref[hh, :, qs],
                                           preferred_element_type=F32)
    gates = [_nt_dot(kmh, qm) + _nt_dot(kmm, qm) + _nt_dot(kml, qm) for qm in qms]
    g = jnp.where(brow < n_q, jnp.concatenate(gates, axis=1), NEG_INF)
    for t in range(MOBA_TOPK):
        mx = jnp.max(g, axis=0, keepdims=True)
        idx = jnp.min(jnp.where(g == mx, brow_f, float(nblk)), axis=0, keepdims=True)
        sel_ref[pl.ds(t, 1), :] = jnp.where(t < n_q, idx, -1.0)
        g = jnp.where(brow_f == idx, -jnp.inf, g)

    kk = lax.broadcasted_iota(jnp.int32, (BLK, QT), 0)
    tt = lax.broadcasted_iota(jnp.int32, (BLK, QT), 1) % BLK
    for hh in range(NH):
        r = jnp.where(kk <= tt, ra_ref[0, hh], NEG_INF)
        m = jnp.max(r, axis=0, keepdims=True)
        pb_ref[0, hh] = jnp.exp2(r - m).astype(BF16)
        m_ref[hh] = m
    for hh in range(NH):
        for qb in range(QB):
            qs = slice(qb * BLK, (qb + 1) * BLK)
            acc_ref[hh, :, qs] = jnp.dot(vt_ref[n0 + qb, hh], pb_ref[0, hh, :, qs],
                                         preferred_element_type=F32)

    last = n_past - 1

    def step(i, r_new, r_old, p_new, p_old, al_new, al_old):
        def scores():
            for k in range(PAIR):
                a = jnp.minimum(PAIR * i + k, last)
                for hh in range(NH):
                    r_new[k, hh] = jnp.dot(k2_ref[a, hh], q2t_ref[hh], preferred_element_type=F32)

        def pv():
            for hh in range(NH):
                acc = acc_ref[hh]
                for k in range(PAIR):
                    c = jnp.clip(PAIR * (i - 2) + k, 0, last)
                    acc = al_new[k, hh] * acc + jnp.dot(vt_ref[c, hh], p_new[k, hh],
                                                        preferred_element_type=F32)
                acc_ref[hh] = acc

        def softmax():
            for hh in range(NH):
                qs = slice(hh * QT, (hh + 1) * QT)
                m = m_ref[hh]
                for k in range(PAIR):
                    b = PAIR * (i - 1) + k
                    bf = jnp.where((b >= 0) & (b < n_past), b, -2).astype(F32)
                    r = r_old[k, hh]
                    chosen = ((sel_ref[0:1, qs] == bf) | (sel_ref[1:2, qs] == bf)
                              | (sel_ref[2:3, qs] == bf))
                    dist = (n_q_f[:, qs] - b.astype(F32)) * float(BLK)
                    off = jnp.where(chosen, -slopes[hh] * dist, NEG_INF)
                    m_new = jnp.maximum(m, jnp.max(r, axis=0, keepdims=True) + off)
                    al_old[k, hh] = jnp.exp2(m - m_new)
                    p_old[k, hh] = jnp.exp2(r - (m_new - off)).astype(BF16)
                    m = m_new
                m_ref[hh] = m

        for stage in (pv, softmax, scores):
            stage()

    def steps(first, count):
        for u in range(0, count, 2):
            step(first + u, ra_ref, rb_ref, pa_ref, pb_ref, aa_ref, ab_ref)
            step(first + u + 1, rb_ref, ra_ref, pb_ref, pa_ref, ab_ref, aa_ref)

    def body(t, carry):
        steps(MOBA_STEPS_PER_BODY * t, MOBA_STEPS_PER_BODY)
        return carry

    n_steps = (n_past + PAIR - 1) // PAIR + 2
    n_bodies = n_steps // MOBA_STEPS_PER_BODY
    lax.fori_loop(0, n_bodies, body, 0)
    rest = n_steps - n_bodies * MOBA_STEPS_PER_BODY
    for count in range(2, MOBA_STEPS_PER_BODY + 1, 2):
        @pl.when((rest > count - 2) & (rest <= count))
        def _(count=count):
            steps(n_bodies * MOBA_STEPS_PER_BODY, count)
    outs = []
    for hh in range(NH):
        a = acc_ref[hh]
        outs.append(a[:HEAD_DIM, :] / a[HEAD_DIM:HEAD_DIM + 1, :])
    o_ref[0] = jnp.concatenate(outs, axis=0).T.astype(BF16)


def _moba(proj, B, S):
    nblk = S // MOBA_BLOCK
    nh = HEADS_PER_LANE_TILE
    n_hp = MOBA_HEADS // nh
    qt = MOBA_QBLOCKS * MOBA_BLOCK
    cq, ck, cv = COL_BQ // LANES, COL_BK // LANES, COL_BV // LANES
    return pl.pallas_call(
        functools.partial(_moba_kernel, nblk=nblk),
        grid=(B, n_hp, S // qt),
        in_specs=[
            pl.BlockSpec((1, qt, LANES), lambda b, h, n: (b, n, cq + h)),
            pl.BlockSpec((1, S, LANES), lambda b, h, n: (b, 0, ck + h), pipeline_mode=pl.Buffered(1)),
            pl.BlockSpec((1, S, LANES), lambda b, h, n: (b, 0, cv + h), pipeline_mode=pl.Buffered(1)),
        ],
        out_specs=pl.BlockSpec((1, qt, LANES), lambda b, h, n: (b, n, h)),
        out_shape=jax.ShapeDtypeStruct((B, S, D_B), BF16),
        scratch_shapes=[pltpu.VMEM((nblk, LANES), F32)] * 3 + [
            pltpu.VMEM((nblk, nh, MOBA_BLOCK, LANES), BF16),
            pltpu.VMEM((nblk, nh, MOBA_VROWS, MOBA_BLOCK), BF16),
            pltpu.VMEM((nh, LANES, qt), BF16),
            pltpu.VMEM((8, nh * qt), F32),
            pltpu.VMEM((nh, 1, qt), F32),
            pltpu.VMEM((nh, MOBA_VROWS, qt), F32),
        ] + [pltpu.VMEM((MOBA_PAIR, nh, MOBA_BLOCK, qt), F32)] * 2
        + [pltpu.VMEM((MOBA_PAIR, nh, MOBA_BLOCK, qt), BF16)] * 2
        + [pltpu.VMEM((MOBA_PAIR, nh, 1, qt), F32)] * 2,
        compiler_params=pltpu.CompilerParams(
            dimension_semantics=("parallel", "parallel", "arbitrary"), vmem_limit_bytes=VMEM_LIMIT),
        name="moba",
    )(proj, proj, proj)


def _memkv_kernel(m_ref, nw_ref, w_ref, o_ref):
    h = _rmsnorm_bf16(m_ref[...], nw_ref[...])
    o_ref[...] = jnp.dot(h, w_ref[...], preferred_element_type=F32).astype(BF16)


def _memkv(mem2, mem_norm_w, w_bf16):
    R = mem2.shape[0]
    return pl.pallas_call(
        _memkv_kernel,
        out_shape=jax.ShapeDtypeStruct((R, 2 * D_M), BF16),
        compiler_params=pltpu.CompilerParams(vmem_limit_bytes=VMEM_LIMIT),
        name="memkv",
    )(mem2, mem_norm_w, w_bf16)


def _silu(x):
    return x * jax.nn.sigmoid(x)


def _merge_kernel(x_ref, o1_ref, o2_ref, o3_ref, l1_ref, l2_ref, l3_ref, ob_ref, ga_ref, gb_ref,
                  mq_ref, mg_ref, lg_ref, kv_ref, bm_ref, wa_ref, wb_ref, wm_ref, wo_ref, fw_ref,
                  out_ref):
    tm = x_ref.shape[1]
    l1, l2, l3 = l1_ref[0], l2_ref[0], l3_ref[0]
    lmax = jnp.maximum(jnp.maximum(l1, l2), l3)
    e1, e2, e3 = jnp.exp(l1 - lmax), jnp.exp(l2 - lmax), jnp.exp(l3 - lmax)
    inv = 1.0 / (e1 + e2 + e3)
    erow = lax.broadcasted_iota(jnp.int32, (LANES, D_A), 0)
    ecol = lax.broadcasted_iota(jnp.int32, (LANES, D_A), 1)
    expand = jnp.where((ecol >= erow * HEAD_DIM) & (ecol < (erow + 1) * HEAD_DIM), 1.0, 0.0).astype(BF16)

    def per_head(w):
        hi, mid, _ = _split3(w)
        return (jnp.dot(hi, expand, preferred_element_type=F32)
                + jnp.dot(mid, expand, preferred_element_type=F32))

    def merge_gate(k):
        cs = slice(k * D_MODEL, (k + 1) * D_MODEL)
        return jax.nn.sigmoid(lg_ref[0, :, cs].astype(F32) + bm_ref[:, cs])

    o_a = (per_head(e1 * inv) * o1_ref[0].astype(F32) + per_head(e2 * inv) * o2_ref[0].astype(F32)
           + per_head(e3 * inv) * o3_ref[0].astype(F32))
    t_a = (o_a * _silu(ga_ref[0].astype(F32))).astype(BF16)
    merged = merge_gate(0) * jnp.dot(t_a, wa_ref[...], preferred_element_type=F32)

    t_b = (ob_ref[0].astype(F32) * _silu(gb_ref[0].astype(F32))).astype(BF16)
    merged = merged + merge_gate(1) * jnp.dot(t_b, wb_ref[...], preferred_element_type=F32)

    mq = mq_ref[0] * ATTN_SCALE
    km = kv_ref[0, :, :D_M]
    vm = kv_ref[0, :, D_M:]
    mlane = lax.broadcasted_iota(jnp.int32, (tm, D_M), 1)
    o_m = jnp.zeros((tm, D_M), F32)
    for h in range(MEM_HEADS):
        in_head = (mlane >= h * HEAD_DIM) & (mlane < (h + 1) * HEAD_DIM)
        s = _nt_dot(jnp.where(in_head, mq, jnp.zeros_like(mq)), km)
        s = s - jnp.max(s, axis=-1, keepdims=True)
        e = jnp.exp(s)
        p = (e * (1.0 / jnp.sum(e, axis=-1, keepdims=True))).astype(BF16)
        o_m = jnp.where(in_head, jnp.dot(p, vm, preferred_element_type=F32), o_m)
    t_m = (o_m * _silu(mg_ref[0].astype(F32))).astype(BF16)
    merged = merged + merge_gate(2) * jnp.dot(t_m, wm_ref[...], preferred_element_type=F32)

    y = x_ref[0] + jnp.dot(merged.astype(BF16), wo_ref[...], preferred_element_type=F32)
    ms = jnp.mean(y * y, axis=-1, keepdims=True)
    out_ref[0] = y * lax.rsqrt(ms + EPS) * fw_ref[...]


def _merge(x, o_groups, lse_groups, o_b, proj, kv, b_merge, wa, wb, wm, wo, fw, tm=256):
    B, S, _ = x.shape

    def tok(width, colblk):
        return pl.BlockSpec((1, tm, width), lambda b, i: (b, i, colblk))

    def full(shape):
        return pl.BlockSpec(shape, lambda b, i: (0,) * len(shape))

    in_specs = (
        [tok(D_MODEL, 0)]
        + [tok(D_A, 0)] * 3
        + [tok(LANES, 0)] * 3
        + [tok(D_B, 0)]
        + [tok(D_A, COL_A_GATE // D_A), tok(D_B, COL_B_GATE // D_B),
           tok(D_M, COL_MQ // D_M), tok(D_M, COL_M_GATE // D_M), tok(D_MERGE, COL_MERGE // D_MERGE)]
        + [pl.BlockSpec((1, kv.shape[1], 2 * D_M), lambda b, i: (b, 0, 0))]
        + [full((1, D_MERGE)), full((D_A, D_MODEL)), full((D_B, D_MODEL)), full((D_M, D_MODEL)),
           full((D_MODEL, D_MODEL)), full((1, D_MODEL))]
    )
    return pl.pallas_call(
        _merge_kernel,
        grid=(B, S // tm),
        in_specs=in_specs,
        out_specs=tok(D_MODEL, 0),
        out_shape=jax.ShapeDtypeStruct((B, S, D_MODEL), F32),
        compiler_params=pltpu.CompilerParams(
            dimension_semantics=("parallel", "parallel"), vmem_limit_bytes=VMEM_LIMIT),
        name="merge",
    )(x, *o_groups, *lse_groups, o_b, proj, proj, proj, proj, proj, kv, b_merge, wa, wb, wm, wo, fw)


def kernel(x, mem, norm_w, mem_norm_w, w_in, b_merge, w_mem_kv, w_branch_a, w_branch_b,
           w_branch_m, w_out, final_norm_w):
    B, S, D = x.shape
    assert w_in.shape[0] == 1, "single-layer block"
    w = w_in[0]
    nw = norm_w[0][None, :]
    src_a_gate = N_GROUPS * D_QKV
    src_bq = src_a_gate + D_A
    src_bk = src_bq + D_B
    src_merge = src_bk + 3 * D_B + 2 * D_M
    w_bq = w[:, src_bq:src_bk] * (ATTN_SCALE * LOG2E)
    w_main = jnp.concatenate(
        [w[:, src_merge:], w[:, :D_QKV], w[:, src_a_gate:src_bq], w_bq, w[:, src_bk:src_merge]],
        axis=1).astype(BF16)
    proj = _inproj(x.reshape(B * S, D), nw, w_main, "inproj").reshape(B, S, D_MAIN)

    o_groups, lse_groups = [], []
    for g, (_, dilation) in enumerate(DIL_PAIRS):
        if dilation == 1:
            qkv, col0 = proj.reshape(B, 1, S, D_MAIN), COL_A0 // D_A
        else:
            x_g = x.reshape(B, S // dilation, dilation, D).transpose(0, 2, 1, 3).reshape(B * S, D)
            w_g = w[:, g * D_QKV:(g + 1) * D_QKV].astype(BF16)
            qkv = _inproj(x_g, nw, w_g, f"inproj_d{dilation}").reshape(B, dilation, S // dilation, D_QKV)
            col0 = 0
        o_g, lse_g = _dilated_group(qkv, col0, g, B, S)
        o_groups.append(o_g)
        lse_groups.append(lse_g)
    o_b = _moba(proj, B, S)
    kv = _memkv(mem.reshape(-1, D), mem_norm_w[0][None, :], w_mem_kv[0].astype(BF16))
    kv = kv.reshape(B, mem.shape[1], 2 * D_M)
    return _merge(x, o_groups, lse_groups, o_b, proj, kv, b_merge[0][None, :],
                  w_branch_a[0].astype(BF16), w_branch_b[0].astype(BF16),
                  w_branch_m[0].astype(BF16), w_out[0].astype(BF16), final_norm_w[None, :])
```

```python
import functools
import math

import jax
import jax.numpy as jnp
from jax import lax
from jax.experimental import pallas as pl
from jax.experimental.pallas import tpu as pltpu

F32 = jnp.float32
BF16 = jnp.bfloat16

D_MODEL = 1024
HEAD_DIM = 64
ATTN_SCALE = HEAD_DIM ** -0.5
LOG2E = math.log2(math.e)
DIL_PAIRS = ((128, 1), (512, 4), (2048, 16))
N_GROUPS = len(DIL_PAIRS)
DIL_HEADS = 8
DIL_QBLK = 128
DIL_QBLOCKS = 4
MOBA_HEADS = 8
MOBA_BLOCK = 256
MOBA_TOPK = 3
MEM_HEADS = 4
N_BRANCHES = 3
EPS = 1e-6
NEG_INF = -1e30

D_A = DIL_HEADS * HEAD_DIM
D_B = MOBA_HEADS * HEAD_DIM
D_M = MEM_HEADS * HEAD_DIM
D_MERGE = N_BRANCHES * D_MODEL
D_QKV = 3 * D_A

COL_MERGE = 0
COL_A0 = D_MERGE
COL_A_GATE = COL_A0 + D_QKV
COL_BQ = COL_A_GATE + D_A
COL_BK = COL_BQ + D_B
COL_BV = COL_BK + D_B
COL_B_GATE = COL_BV + D_B
COL_MQ = COL_B_GATE + D_B
COL_M_GATE = COL_MQ + D_M
D_MAIN = COL_M_GATE + D_M

LANES = 128
HEADS_PER_LANE_TILE = LANES // HEAD_DIM

SLOPES_A = tuple(2.0 ** (-8.0 * k / 16) for k in range(1, 17, 2))
SLOPES_B = tuple(2.0 ** (-8.0 * k / 16) for k in range(2, 17, 2))

VMEM_LIMIT = 48 * 1024 * 1024


def _nt_dot(a, b):
    return lax.dot_general(a, b, (((1,), (1,)), ((), ())), preferred_element_type=F32)


def _split3(v):
    hi = v.astype(BF16)
    r1 = v - hi.astype(F32)
    mid = r1.astype(BF16)
    lo = (r1 - mid.astype(F32)).astype(BF16)
    return hi, mid, lo


def _rmsnorm_bf16(xf, w):
    ms = jnp.mean(xf * xf, axis=-1, keepdims=True)
    return (xf * lax.rsqrt(ms + EPS) * w).astype(BF16)


def _inproj_kernel(x_ref, nw_ref, w_ref, o_ref, h_ref):
    @pl.when(pl.program_id(1) == 0)
    def _():
        h_ref[...] = _rmsnorm_bf16(x_ref[...], nw_ref[...])

    o_ref[...] = jnp.dot(h_ref[...], w_ref[...], preferred_element_type=F32).astype(BF16)


def _inproj(x2, norm_w, w_bf16, name, tm=1024, tn=1536):
    T = x2.shape[0]
    n_cols = w_bf16.shape[1]
    return pl.pallas_call(
        _inproj_kernel,
        grid=(T // tm, n_cols // tn),
        in_specs=[
            pl.BlockSpec((tm, D_MODEL), lambda i, j: (i, 0)),
            pl.BlockSpec((1, D_MODEL), lambda i, j: (0, 0)),
            pl.BlockSpec((D_MODEL, tn), lambda i, j: (0, j)),
        ],
        out_specs=pl.BlockSpec((tm, tn), lambda i, j: (i, j)),
        out_shape=jax.ShapeDtypeStruct((T, n_cols), BF16),
        scratch_shapes=[pltpu.VMEM((tm, D_MODEL), BF16)],
        compiler_params=pltpu.CompilerParams(
            dimension_semantics=("parallel", "arbitrary"), vmem_limit_bytes=VMEM_LIMIT),
        name=name,
    )(x2, norm_w, w_bf16)


def _dilated_kernel(q_ref, kp_ref, kc_ref, vp_ref, vc_ref, o_ref, lse_ref, bias_ref, s_ref, p_ref,
                    *, dilation, span):
    blk = pl.program_id(2)
    Q = DIL_QBLK
    NH = HEADS_PER_LANE_TILE
    n_tiles = DIL_HEADS // NH

    @pl.when((pl.program_id(0) == 0) & (pl.program_id(1) == 0) & (blk == 0))
    def _():
        row = lax.broadcasted_iota(jnp.int32, (Q, 2 * Q), 0)
        col = lax.broadcasted_iota(jnp.int32, (Q, 2 * Q), 1)
        steps = row + Q - col
        in_band = (steps >= 0) & (steps <= span)
        dist = (steps * dilation).astype(F32)
        for h in range(DIL_HEADS):
            bias = jnp.where(in_band, -SLOPES_A[h] * dist, NEG_INF)
            bias_ref[0, h] = bias
            bias_ref[1, h] = jnp.where(col >= Q, bias, NEG_INF)

    lane = lax.broadcasted_iota(jnp.int32, (Q, LANES), 1)
    in_head = [(lane >= hh * HEAD_DIM) & (lane < (hh + 1) * HEAD_DIM) for hh in range(NH)]

    for qb in range(DIL_QBLOCKS):
        rows = slice(qb * Q, (qb + 1) * Q)
        variant = (blk == 0).astype(jnp.int32) if qb == 0 else 0

        def band(prev_ref, cur_ref, ls):
            if qb == 0:
                return jnp.concatenate([prev_ref[0, 0, :, ls], cur_ref[0, 0, :Q, ls]], axis=0)
            return cur_ref[0, 0, (qb - 1) * Q:(qb + 1) * Q, ls]

        for t in range(n_tiles):
            ls = slice(t * LANES, (t + 1) * LANES)
            q = q_ref[0, 0, rows, ls] * ATTN_SCALE
            k = band(kp_ref, kc_ref, ls)
            for hh in range(NH):
                s_ref[NH * t + hh] = _nt_dot(jnp.where(in_head[hh], q, jnp.zeros_like(q)), k)

        lse_tile = jnp.zeros((Q, LANES), F32)
        inv_den = []
        for h in range(DIL_HEADS):
            s = s_ref[h] + bias_ref[variant, h]
            m = jnp.max(s, axis=-1, keepdims=True)
            e = jnp.exp(s - m)
            den = jnp.sum(e, axis=-1, keepdims=True)
            p_ref[h] = e.astype(BF16)
            inv_den.append(1.0 / den)
            lse_tile = jnp.where(lane == h, m + jnp.log(den), lse_tile)
        lse_ref[0, rows, :] = lse_tile

        for t in range(n_tiles):
            ls = slice(t * LANES, (t + 1) * LANES)
            v = band(vp_ref, vc_ref, ls)
            o = [jnp.dot(p_ref[NH * t + hh], v, preferred_element_type=F32) * inv_den[NH * t + hh]
                 for hh in range(NH)]
            o_ref[0, rows, ls] = jnp.where(in_head[0], o[0], o[1]).astype(BF16)


def _dilated_group(qkv, col0, g, B, S):
    window, dilation = DIL_PAIRS[g]
    n = S // dilation
    nb = n // DIL_QBLK
    span = window // dilation

    rows = DIL_QBLOCKS * DIL_QBLK

    def spec(piece, prev):
        if prev:
            return pl.BlockSpec((1, 1, DIL_QBLK, D_A),
                                lambda b, r, i: (b, r, jnp.maximum(DIL_QBLOCKS * i - 1, 0), col0 + piece))
        return pl.BlockSpec((1, 1, rows, D_A), lambda b, r, i: (b, r, i, col0 + piece))

    o, lse = pl.pallas_call(
        functools.partial(_dilated_kernel, dilation=dilation, span=span),
        grid=(B, dilation, nb // DIL_QBLOCKS),
        in_specs=[spec(0, False), spec(1, True), spec(1, False), spec(2, True), spec(2, False)],
        out_specs=[
            pl.BlockSpec((1, rows, D_A), lambda b, r, i: (b, i, r)),
            pl.BlockSpec((1, rows, LANES), lambda b, r, i: (b, i, r)),
        ],
        out_shape=[
            jax.ShapeDtypeStruct((B, n, dilation * D_A), BF16),
            jax.ShapeDtypeStruct((B, n, dilation * LANES), F32),
        ],
        scratch_shapes=[
            pltpu.VMEM((2, DIL_HEADS, DIL_QBLK, 2 * DIL_QBLK), F32),
            pltpu.VMEM((DIL_HEADS, DIL_QBLK, 2 * DIL_QBLK), F32),
            pltpu.VMEM((DIL_HEADS, DIL_QBLK, 2 * DIL_QBLK), BF16),
        ],
        compiler_params=pltpu.CompilerParams(
            dimension_semantics=("arbitrary", "arbitrary", "arbitrary"), vmem_limit_bytes=VMEM_LIMIT),
        name=f"dilated_g{g}",
    )(qkv, qkv, qkv, qkv, qkv)
    return o, lse


MOBA_PAIR = 1
MOBA_QBLOCKS = 2
MOBA_STEPS_PER_BODY = 8
MOBA_VROWS = HEAD_DIM + 16
SLOPES_B_LOG2 = tuple(s * LOG2E for s in SLOPES_B)


def _moba_kernel(q_ref, k_ref, v_ref, o_ref, kmh_ref, kmm_ref, kml_ref, k2_ref, vt_ref, q2t_ref,
                 sel_ref, m_ref, acc_ref, ra_ref, rb_ref, pa_ref, pb_ref, aa_ref, ab_ref, *, nblk):
    hp = pl.program_id(1)
    tile = pl.program_id(2)
    BLK = MOBA_BLOCK
    NH = HEADS_PER_LANE_TILE
    PAIR = MOBA_PAIR
    QB = MOBA_QBLOCKS
    QT = QB * BLK
    ALIBI_LANE = (HEAD_DIM, 0)
    n0 = tile * QB
    n_past = n0 + QB - 1

    def lane_masks(rows):
        lane = lax.broadcasted_iota(jnp.int32, (rows, LANES), 1)
        in_head = [(lane >= hh * HEAD_DIM) & (lane < (hh + 1) * HEAD_DIM) for hh in range(NH)]
        alibi = [[lane == ALIBI_LANE[hh] + t for t in range(3)] for hh in range(NH)]
        return in_head, alibi

    slopes = []
    for hh in range(NH):
        slope = jnp.float32(SLOPES_B_LOG2[hh])
        for p in range(1, MOBA_HEADS // NH):
            slope = jnp.where(hp == p, jnp.float32(SLOPES_B_LOG2[NH * p + hh]), slope)
        slopes.append(slope)

    @pl.when(tile == 0)
    def _():
        in_head, alibi = lane_masks(BLK)
        key_off = lax.broadcasted_iota(jnp.int32, (BLK, LANES), 0).astype(F32)
        ones_tile = jnp.where(lax.broadcasted_iota(jnp.int32, (16, BLK), 0) == 0, 1.0, 0.0)
        cbs = []
        for hh in range(NH):
            terms = _split3(slopes[hh] * key_off)
            cb = jnp.zeros((BLK, LANES), F32)
            for t in range(3):
                cb = jnp.where(alibi[hh][t], terms[t].astype(F32), cb)
            cbs.append(cb)

        def body(j, c):
            rows = pl.ds(pl.multiple_of(j * BLK, BLK), BLK)
            kb = k_ref[0, rows, :].astype(F32)
            km = jnp.sum(kb, axis=0, keepdims=True) * (1.0 / BLK)
            hi, mid, lo = _split3(km)
            kmh_ref[pl.ds(j, 1), :] = hi.astype(F32)
            kmm_ref[pl.ds(j, 1), :] = mid.astype(F32)
            kml_ref[pl.ds(j, 1), :] = lo.astype(F32)
            vt = v_ref[0, rows, :].astype(F32).T
            for hh in range(NH):
                k2_ref[j, hh] = jnp.where(in_head[hh], kb, cbs[hh]).astype(BF16)
                vh = vt[hh * HEAD_DIM:(hh + 1) * HEAD_DIM, :]
                vt_ref[j, hh] = jnp.concatenate([vh, ones_tile], axis=0).astype(BF16)
            return c
        lax.fori_loop(0, nblk, body, 0)
        rb_ref[...] = jnp.zeros((PAIR, NH, BLK, QT), F32)
        pa_ref[...] = jnp.zeros((PAIR, NH, BLK, QT), BF16)
        aa_ref[...] = jnp.ones((PAIR, NH, 1, QT), F32)

    in_head_q, alibi_q = lane_masks(QT)
    qpos = lax.broadcasted_iota(jnp.int32, (1, NH * QT), 1)
    n_q = n0 + (qpos % QT) // BLK
    n_q_f = n_q.astype(F32)
    brow = lax.broadcasted_iota(jnp.int32, (nblk, NH * QT), 0)
    brow_f = brow.astype(F32)

    q_all = q_ref[0]
    kmh = kmh_ref[...].astype(BF16)
    kmm = kmm_ref[...].astype(BF16)
    kml = kml_ref[...].astype(BF16)

    qms = []
    for hh in range(NH):
        qm = jnp.where(in_head_q[hh], q_all, jnp.zeros_like(q_all))
        is_alibi = alibi_q[hh][0] | alibi_q[hh][1] | alibi_q[hh][2]
        q2t_ref[hh] = jnp.where(is_alibi, 1.0, qm.astype(F32)).T.astype(BF16)
        qms.append(qm)
    for hh in range(NH):
        for qb in range(QB):
            qs = slice(qb * BLK, (qb + 1) * BLK)
            ra_ref[0, hh, :, qs] = jnp.dot(k2_ref[n0 + qb, hh], q2t_ref[hh, :, qs],
                                           preferred_element_type=F32)
    gates = [_nt_dot(kmh, qm) + _nt_dot(kmm, qm) + _nt_dot(kml, qm) for qm in qms]
    g = jnp.where(brow < n_q, jnp.concatenate(gates, axis=1), NEG_INF)
    for t in range(MOBA_TOPK):
        mx = jnp.max(g, axis=0, keepdims=True)
        idx = jnp.min(jnp.where(g == mx, brow_f, float(nblk)), axis=0, keepdims=True)
        sel_ref[pl.ds(t, 1), :] = jnp.where(t < n_q, idx, -1.0)
        g = jnp.where(brow_f == idx, -jnp.inf, g)

    kk = lax.broadcasted_iota(jnp.int32, (BLK, QT), 0)
    tt = lax.broadcasted_iota(jnp.int32, (BLK, QT), 1) % BLK
    for hh in range(NH):
        r = jnp.where(kk <= tt, ra_ref[0, hh], NEG_INF)
        m = jnp.max(r, axis=0, keepdims=True)
        pb_ref[0, hh] = jnp.exp2(r - m).astype(BF16)
        m_ref[hh] = m
    for hh in range(NH):
        for qb in range(QB):
            qs = slice(qb * BLK, (qb + 1) * BLK)
            acc_ref[hh, :, qs] = jnp.dot(vt_ref[n0 + qb, hh], pb_ref[0, hh, :, qs],
                                         preferred_element_type=F32)

    last = n_past - 1

    def step(i, r_new, r_old, p_new, p_old, al_new, al_old):
        def scores():
            for k in range(PAIR):
                a = jnp.minimum(PAIR * i + k, last)
                for hh in range(NH):
                    r_new[k, hh] = jnp.dot(k2_ref[a, hh], q2t_ref[hh], preferred_element_type=F32)

        def pv():
            for hh in range(NH):
                acc = acc_ref[hh]
                for k in range(PAIR):
                    c = jnp.clip(PAIR * (i - 2) + k, 0, last)
                    acc = al_new[k, hh] * acc + jnp.dot(vt_ref[c, hh], p_new[k, hh],
                                                        preferred_element_type=F32)
                acc_ref[hh] = acc

        def softmax():
            for hh in range(NH):
                qs = slice(hh * QT, (hh + 1) * QT)
                m = m_ref[hh]
                for k in range(PAIR):
                    b = PAIR * (i - 1) + k
                    bf = jnp.where((b >= 0) & (b < n_past), b, -2).astype(F32)
                    r = r_old[k, hh]
                    chosen = ((sel_ref[0:1, qs] == bf) | (sel_ref[1:2, qs] == bf)
                              | (sel_ref[2:3, qs] == bf))
                    dist = (n_q_f[:, qs] - b.astype(F32)) * float(BLK)
                    off = jnp.where(chosen, -slopes[hh] * dist, NEG_INF)
                    m_new = jnp.maximum(m, jnp.max(r, axis=0, keepdims=True) + off)
                    al_old[k, hh] = jnp.exp2(m - m_new)
                    p_old[k, hh] = jnp.exp2(r - (m_new - off)).astype(BF16)
                    m = m_new
                m_ref[hh] = m

        for stage in (pv, softmax, scores):
            stage()

    def steps(first, count):
        for u in range(0, count, 2):
            step(first + u, ra_ref, rb_ref, pa_ref, pb_ref, aa_ref, ab_ref)
            step(first + u + 1, rb_ref, ra_ref, pb_ref, pa_ref, ab_ref, aa_ref)

    def body(t, carry):
        steps(MOBA_STEPS_PER_BODY * t, MOBA_STEPS_PER_BODY)
        return carry

    n_steps = (n_past + PAIR - 1) // PAIR + 2
    n_bodies = n_steps // MOBA_STEPS_PER_BODY
    lax.fori_loop(0, n_bodies, body, 0)
    rest = n_steps - n_bodies * MOBA_STEPS_PER_BODY
    for count in range(2, MOBA_STEPS_PER_BODY + 1, 2):
        @pl.when((rest > count - 2) & (rest <= count))
        def _(count=count):
            steps(n_bodies * MOBA_STEPS_PER_BODY, count)
    outs = []
    for hh in range(NH):
        a = acc_ref[hh]
        outs.append(a[:HEAD_DIM, :] / a[HEAD_DIM:HEAD_DIM + 1, :])
    o_ref[0] = jnp.concatenate(outs, axis=0).T.astype(BF16)


def _moba(proj, B, S):
    nblk = S // MOBA_BLOCK
    nh = HEADS_PER_LANE_TILE
    n_hp = MOBA_HEADS // nh
    qt = MOBA_QBLOCKS * MOBA_BLOCK
    cq, ck, cv = COL_BQ // LANES, COL_BK // LANES, COL_BV // LANES
    return pl.pallas_call(
        functools.partial(_moba_kernel, nblk=nblk),
        grid=(B, n_hp, S // qt),
        in_specs=[
            pl.BlockSpec((1, qt, LANES), lambda b, h, n: (b, n, cq + h)),
            pl.BlockSpec((1, S, LANES), lambda b, h, n: (b, 0, ck + h), pipeline_mode=pl.Buffered(1)),
            pl.BlockSpec((1, S, LANES), lambda b, h, n: (b, 0, cv + h), pipeline_mode=pl.Buffered(1)),
        ],
        out_specs=pl.BlockSpec((1, qt, LANES), lambda b, h, n: (b, n, h)),
        out_shape=jax.ShapeDtypeStruct((B, S, D_B), BF16),
        scratch_shapes=[pltpu.VMEM((nblk, LANES), F32)] * 3 + [
            pltpu.VMEM((nblk, nh, MOBA_BLOCK, LANES), BF16),
            pltpu.VMEM((nblk, nh, MOBA_VROWS, MOBA_BLOCK), BF16),
            pltpu.VMEM((nh, LANES, qt), BF16),
            pltpu.VMEM((8, nh * qt), F32),
            pltpu.VMEM((nh, 1, qt), F32),
            pltpu.VMEM((nh, MOBA_VROWS, qt), F32),
        ] + [pltpu.VMEM((MOBA_PAIR, nh, MOBA_BLOCK, qt), F32)] * 2
        + [pltpu.VMEM((MOBA_PAIR, nh, MOBA_BLOCK, qt), BF16)] * 2
        + [pltpu.VMEM((MOBA_PAIR, nh, 1, qt), F32)] * 2,
        compiler_params=pltpu.CompilerParams(
            dimension_semantics=("parallel", "parallel", "arbitrary"), vmem_limit_bytes=VMEM_LIMIT),
        name="moba",
    )(proj, proj, proj)


def _memkv_kernel(m_ref, nw_ref, w_ref, o_ref):
    h = _rmsnorm_bf16(m_ref[...], nw_ref[...])
    o_ref[...] = jnp.dot(h, w_ref[...], preferred_element_type=F32).astype(BF16)


def _memkv(mem2, mem_norm_w, w_bf16):
    R = mem2.shape[0]
    return pl.pallas_call(
        _memkv_kernel,
        out_shape=jax.ShapeDtypeStruct((R, 2 * D_M), BF16),
        compiler_params=pltpu.CompilerParams(vmem_limit_bytes=VMEM_LIMIT),
        name="memkv",
    )(mem2, mem_norm_w, w_bf16)


def _silu(x):
    return x * jax.nn.sigmoid(x)


def _merge_kernel(x_ref, o1_ref, o2_ref, o3_ref, l1_ref, l2_ref, l3_ref, ob_ref, ga_ref, gb_ref,
                  mq_ref, mg_ref, lg_ref, kv_ref, bm_ref, wa_ref, wb_ref, wm_ref, wo_ref, fw_ref,
                  out_ref):
    tm = x_ref.shape[1]

    def natural(ref, width, exact_f32):
        d = ref.shape[2] // width
        if d == 1:
            return ref[0].astype(F32)
        rows = tm // d
        stacked = jnp.concatenate([ref[0, :, r * width:(r + 1) * width] for r in range(d)], axis=0)
        t = lax.broadcasted_iota(jnp.int32, (tm, tm), 0)
        src = lax.broadcasted_iota(jnp.int32, (tm, tm), 1)
        perm = jnp.where((t % d) * rows + t // d == src, 1.0, 0.0).astype(BF16)
        if not exact_f32:
            return jnp.dot(perm, stacked, preferred_element_type=F32)
        hi, mid, _ = _split3(stacked)
        return (jnp.dot(perm, hi, preferred_element_type=F32)
                + jnp.dot(perm, mid, preferred_element_type=F32))

    l1, l2, l3 = (natural(r, LANES, True) for r in (l1_ref, l2_ref, l3_ref))
    lmax = jnp.maximum(jnp.maximum(l1, l2), l3)
    e1, e2, e3 = jnp.exp(l1 - lmax), jnp.exp(l2 - lmax), jnp.exp(l3 - lmax)
    inv = 1.0 / (e1 + e2 + e3)
    erow = lax.broadcasted_iota(jnp.int32, (LANES, D_A), 0)
    ecol = lax.broadcasted_iota(jnp.int32, (LANES, D_A), 1)
    expand = jnp.where((ecol >= erow * HEAD_DIM) & (ecol < (erow + 1) * HEAD_DIM), 1.0, 0.0).astype(BF16)

    def per_head(w):
        hi, mid, _ = _split3(w)
        return (jnp.dot(hi, expand, preferred_element_type=F32)
                + jnp.dot(mid, expand, preferred_element_type=F32))

    def merge_gate(k):
        cs = slice(k * D_MODEL, (k + 1) * D_MODEL)
        return jax.nn.sigmoid(lg_ref[0, :, cs].astype(F32) + bm_ref[:, cs])

    o_a = (per_head(e1 * inv) * natural(o1_ref, D_A, False)
           + per_head(e2 * inv) * natural(o2_ref, D_A, False)
           + per_head(e3 * inv) * natural(o3_ref, D_A, False))
    t_a = (o_a * _silu(ga_ref[0].astype(F32))).astype(BF16)
    merged = merge_gate(0) * jnp.dot(t_a, wa_ref[...], preferred_element_type=F32)

    t_b = (ob_ref[0].astype(F32) * _silu(gb_ref[0].astype(F32))).astype(BF16)
    merged = merged + merge_gate(1) * jnp.dot(t_b, wb_ref[...], preferred_element_type=F32)

    mq = mq_ref[0] * ATTN_SCALE
    km = kv_ref[0, :, :D_M]
    vm = kv_ref[0, :, D_M:]
    mlane = lax.broadcasted_iota(jnp.int32, (tm, D_M), 1)
    o_m = jnp.zeros((tm, D_M), F32)
    for h in range(MEM_HEADS):
        in_head = (mlane >= h * HEAD_DIM) & (mlane < (h + 1) * HEAD_DIM)
        s = _nt_dot(jnp.where(in_head, mq, jnp.zeros_like(mq)), km)
        s = s - jnp.max(s, axis=-1, keepdims=True)
        e = jnp.exp(s)
        p = (e * (1.0 / jnp.sum(e, axis=-1, keepdims=True))).astype(BF16)
        o_m = jnp.where(in_head, jnp.dot(p, vm, preferred_element_type=F32), o_m)
    t_m = (o_m * _silu(mg_ref[0].astype(F32))).astype(BF16)
    merged = merged + merge_gate(2) * jnp.dot(t_m, wm_ref[...], preferred_element_type=F32)

    y = x_ref[0] + jnp.dot(merged.astype(BF16), wo_ref[...], preferred_element_type=F32)
    ms = jnp.mean(y * y, axis=-1, keepdims=True)
    out_ref[0] = y * lax.rsqrt(ms + EPS) * fw_ref[...]


def _merge(x, o_groups, lse_groups, o_b, proj, kv, b_merge, wa, wb, wm, wo, fw, tm=256):
    B, S, _ = x.shape

    def tok(width, colblk):
        return pl.BlockSpec((1, tm, width), lambda b, i: (b, i, colblk))

    def dil(width, d):
        return pl.BlockSpec((1, tm // d, d * width), lambda b, i: (b, i, 0))

    def full(shape):
        return pl.BlockSpec(shape, lambda b, i: (0,) * len(shape))

    in_specs = (
        [tok(D_MODEL, 0)]
        + [dil(D_A, d) for _, d in DIL_PAIRS]
        + [dil(LANES, d) for _, d in DIL_PAIRS]
        + [tok(D_B, 0)]
        + [tok(D_A, COL_A_GATE // D_A), tok(D_B, COL_B_GATE // D_B),
           tok(D_M, COL_MQ // D_M), tok(D_M, COL_M_GATE // D_M), tok(D_MERGE, COL_MERGE // D_MERGE)]
        + [pl.BlockSpec((1, kv.shape[1], 2 * D_M), lambda b, i: (b, 0, 0))]
        + [full((1, D_MERGE)), full((D_A, D_MODEL)), full((D_B, D_MODEL)), full((D_M, D_MODEL)),
           full((D_MODEL, D_MODEL)), full((1, D_MODEL))]
    )
    return pl.pallas_call(
        _merge_kernel,
        grid=(B, S // tm),
        in_specs=in_specs,
        out_specs=tok(D_MODEL, 0),
        out_shape=jax.ShapeDtypeStruct((B, S, D_MODEL), F32),
        compiler_params=pltpu.CompilerParams(
            dimension_semantics=("parallel", "parallel"), vmem_limit_bytes=VMEM_LIMIT),
        name="merge",
    )(x, *o_groups, *lse_groups, o_b, proj, proj, proj, proj, proj, kv, b_merge, wa, wb, wm, wo, fw)


def kernel(x, mem, norm_w, mem_norm_w, w_in, b_merge, w_mem_kv, w_branch_a, w_branch_b,
           w_branch_m, w_out, final_norm_w):
    B, S, D = x.shape
    assert w_in.shape[0] == 1, "single-layer block"
    w = w_in[0]
    nw = norm_w[0][None, :]
    src_a_gate = N_GROUPS * D_QKV
    src_bq = src_a_gate + D_A
    src_bk = src_bq + D_B
    src_merge = src_bk + 3 * D_B + 2 * D_M
    w_bq = w[:, src_bq:src_bk] * (ATTN_SCALE * LOG2E)
    w_main = jnp.concatenate(
        [w[:, src_merge:], w[:, :D_QKV], w[:, src_a_gate:src_bq], w_bq, w[:, src_bk:src_merge]],
        axis=1).astype(BF16)
    proj = _inproj(x.reshape(B * S, D), nw, w_main, "inproj").reshape(B, S, D_MAIN)

    o_groups, lse_groups = [], []
    for g, (_, dilation) in enumerate(DIL_PAIRS):
        if dilation == 1:
            qkv, col0 = proj.reshape(B, 1, S, D_MAIN), COL_A0 // D_A
        else:
            x_g = x.reshape(B, S // dilation, dilation, D).transpose(0, 2, 1, 3).reshape(B * S, D)
            w_g = w[:, g * D_QKV:(g + 1) * D_QKV].astype(BF16)
            qkv = _inproj(x_g, nw, w_g, f"inproj_d{dilation}").reshape(B, dilation, S // dilation, D_QKV)
            col0 = 0
        o_g, lse_g = _dilated_group(qkv, col0, g, B, S)
        o_groups.append(o_g)
        lse_groups.append(lse_g)
    o_b = _moba(proj, B, S)
    kv = _memkv(mem.reshape(-1, D), mem_norm_w[0][None, :], w_mem_kv[0].astype(BF16))
    kv = kv.reshape(B, mem.shape[1], 2 * D_M)
    return _merge(x, o_groups, lse_groups, o_b, proj, kv, b_merge[0][None, :],
                  w_branch_a[0].astype(BF16), w_branch_b[0].astype(BF16),
                  w_branch_m[0].astype(BF16), w_out[0].astype(BF16), final_norm_w[None, :])
```

```python
import functools
import math

import jax
import jax.numpy as jnp
from jax import lax
from jax.experimental import pallas as pl
from jax.experimental.pallas import tpu as pltpu

F32 = jnp.float32
BF16 = jnp.bfloat16

D_MODEL = 1024
HEAD_DIM = 64
ATTN_SCALE = HEAD_DIM ** -0.5
LOG2E = math.log2(math.e)
DIL_PAIRS = ((128, 1), (512, 4), (2048, 16))
N_GROUPS = len(DIL_PAIRS)
DIL_HEADS = 8
DIL_QBLK = 128
DIL_QBLOCKS = 4
MOBA_HEADS = 8
MOBA_BLOCK = 256
MOBA_TOPK = 3
MEM_HEADS = 4
N_BRANCHES = 3
EPS = 1e-6
NEG_INF = -1e30

D_A = DIL_HEADS * HEAD_DIM
D_B = MOBA_HEADS * HEAD_DIM
D_M = MEM_HEADS * HEAD_DIM
D_MERGE = N_BRANCHES * D_MODEL
D_QKV = 3 * D_A

COL_MERGE = 0
COL_A0 = D_MERGE
COL_A_GATE = COL_A0 + D_QKV
COL_BQ = COL_A_GATE + D_A
COL_BK = COL_BQ + D_B
COL_BV = COL_BK + D_B
COL_B_GATE = COL_BV + D_B
COL_MQ = COL_B_GATE + D_B
COL_M_GATE = COL_MQ + D_M
D_MAIN = COL_M_GATE + D_M

LANES = 128
HEADS_PER_LANE_TILE = LANES // HEAD_DIM

SLOPES_A = tuple(2.0 ** (-8.0 * k / 16) for k in range(1, 17, 2))
SLOPES_B = tuple(2.0 ** (-8.0 * k / 16) for k in range(2, 17, 2))

VMEM_LIMIT = 48 * 1024 * 1024


def _nt_dot(a, b):
    return lax.dot_general(a, b, (((1,), (1,)), ((), ())), preferred_element_type=F32)


def _split3(v):
    hi = v.astype(BF16)
    r1 = v - hi.astype(F32)
    mid = r1.astype(BF16)
    lo = (r1 - mid.astype(F32)).astype(BF16)
    return hi, mid, lo


def _rmsnorm_bf16(xf, w):
    ms = jnp.mean(xf * xf, axis=-1, keepdims=True)
    return (xf * lax.rsqrt(ms + EPS) * w).astype(BF16)


def _inproj_kernel(x_ref, nw_ref, w_ref, o_ref, h_ref):
    @pl.when(pl.program_id(1) == 0)
    def _():
        h_ref[...] = _rmsnorm_bf16(x_ref[...], nw_ref[...])

    o_ref[...] = jnp.dot(h_ref[...], w_ref[...], preferred_element_type=F32).astype(BF16)


def _inproj(x2, norm_w, w_bf16, name, col_tiles, tm=1024):
    T = x2.shape[0]
    tn = D_QKV
    n_cols = len(col_tiles) * tn

    def w_tile(j):
        t = col_tiles[0]
        for k, c in enumerate(col_tiles[1:], 1):
            t = jnp.where(j == k, c, t)
        return t

    return pl.pallas_call(
        _inproj_kernel,
        grid=(T // tm, len(col_tiles)),
        in_specs=[
            pl.BlockSpec((tm, D_MODEL), lambda i, j: (i, 0)),
            pl.BlockSpec((1, D_MODEL), lambda i, j: (0, 0)),
            pl.BlockSpec((D_MODEL, tn), lambda i, j: (0, w_tile(j))),
        ],
        out_specs=pl.BlockSpec((tm, tn), lambda i, j: (i, j)),
        out_shape=jax.ShapeDtypeStruct((T, n_cols), BF16),
        scratch_shapes=[pltpu.VMEM((tm, D_MODEL), BF16)],
        compiler_params=pltpu.CompilerParams(
            dimension_semantics=("parallel", "arbitrary"), vmem_limit_bytes=VMEM_LIMIT),
        name=name,
    )(x2, norm_w, w_bf16)


def _dilated_kernel(q_ref, kp_ref, kc_ref, vp_ref, vc_ref, o_ref, lse_ref, bias_ref, s_ref, p_ref,
                    *, dilation, span):
    blk = pl.program_id(2)
    Q = DIL_QBLK
    NH = HEADS_PER_LANE_TILE
    n_tiles = DIL_HEADS // NH

    @pl.when((pl.program_id(0) == 0) & (pl.program_id(1) == 0) & (blk == 0))
    def _():
        row = lax.broadcasted_iota(jnp.int32, (Q, 2 * Q), 0)
        col = lax.broadcasted_iota(jnp.int32, (Q, 2 * Q), 1)
        steps = row + Q - col
        in_band = (steps >= 0) & (steps <= span)
        dist = (steps * dilation).astype(F32)
        for h in range(DIL_HEADS):
            bias = jnp.where(in_band, -SLOPES_A[h] * dist, NEG_INF)
            bias_ref[0, h] = bias
            bias_ref[1, h] = jnp.where(col >= Q, bias, NEG_INF)

    lane = lax.broadcasted_iota(jnp.int32, (Q, LANES), 1)
    in_head = [(lane >= hh * HEAD_DIM) & (lane < (hh + 1) * HEAD_DIM) for hh in range(NH)]

    for qb in range(DIL_QBLOCKS):
        rows = slice(qb * Q, (qb + 1) * Q)
        variant = (blk == 0).astype(jnp.int32) if qb == 0 else 0

        def band(prev_ref, cur_ref, ls):
            if qb == 0:
                return jnp.concatenate([prev_ref[0, 0, :, ls], cur_ref[0, 0, :Q, ls]], axis=0)
            return cur_ref[0, 0, (qb - 1) * Q:(qb + 1) * Q, ls]

        for t in range(n_tiles):
            ls = slice(t * LANES, (t + 1) * LANES)
            q = q_ref[0, 0, rows, ls] * ATTN_SCALE
            k = band(kp_ref, kc_ref, ls)
            for hh in range(NH):
                s_ref[NH * t + hh] = _nt_dot(jnp.where(in_head[hh], q, jnp.zeros_like(q)), k)

        lse_tile = jnp.zeros((Q, LANES), F32)
        inv_den = []
        for h in range(DIL_HEADS):
            s = s_ref[h] + bias_ref[variant, h]
            m = jnp.max(s, axis=-1, keepdims=True)
            e = jnp.exp(s - m)
            den = jnp.sum(e, axis=-1, keepdims=True)
            p_ref[h] = e.astype(BF16)
            inv_den.append(1.0 / den)
            lse_tile = jnp.where(lane == h, m + jnp.log(den), lse_tile)
        lse_ref[0, rows, :] = lse_tile

        for t in range(n_tiles):
            ls = slice(t * LANES, (t + 1) * LANES)
            v = band(vp_ref, vc_ref, ls)
            o = [jnp.dot(p_ref[NH * t + hh], v, preferred_element_type=F32) * inv_den[NH * t + hh]
                 for hh in range(NH)]
            o_ref[0, rows, ls] = jnp.where(in_head[0], o[0], o[1]).astype(BF16)


def _dilated_group(qkv, col0, g, B, S):
    window, dilation = DIL_PAIRS[g]
    n = S // dilation
    nb = n // DIL_QBLK
    span = window // dilation

    rows = DIL_QBLOCKS * DIL_QBLK

    def spec(piece, prev):
        if prev:
            return pl.BlockSpec((1, 1, DIL_QBLK, D_A),
                                lambda b, r, i: (b, r, jnp.maximum(DIL_QBLOCKS * i - 1, 0), col0 + piece))
        return pl.BlockSpec((1, 1, rows, D_A), lambda b, r, i: (b, r, i, col0 + piece))

    o, lse = pl.pallas_call(
        functools.partial(_dilated_kernel, dilation=dilation, span=span),
        grid=(B, dilation, nb // DIL_QBLOCKS),
        in_specs=[spec(0, False), spec(1, True), spec(1, False), spec(2, True), spec(2, False)],
        out_specs=[
            pl.BlockSpec((1, rows, D_A), lambda b, r, i: (b, i, r)),
            pl.BlockSpec((1, rows, LANES), lambda b, r, i: (b, i, r)),
        ],
        out_shape=[
            jax.ShapeDtypeStruct((B, n, dilation * D_A), BF16),
            jax.ShapeDtypeStruct((B, n, dilation * LANES), F32),
        ],
        scratch_shapes=[
            pltpu.VMEM((2, DIL_HEADS, DIL_QBLK, 2 * DIL_QBLK), F32),
            pltpu.VMEM((DIL_HEADS, DIL_QBLK, 2 * DIL_QBLK), F32),
            pltpu.VMEM((DIL_HEADS, DIL_QBLK, 2 * DIL_QBLK), BF16),
        ],
        compiler_params=pltpu.CompilerParams(
            dimension_semantics=("arbitrary", "arbitrary", "arbitrary"), vmem_limit_bytes=VMEM_LIMIT),
        name=f"dilated_g{g}",
    )(qkv, qkv, qkv, qkv, qkv)
    return o, lse


MOBA_PAIR = 1
MOBA_QBLOCKS = 2
MOBA_STEPS_PER_BODY = 8
MOBA_VROWS = HEAD_DIM + 16
SLOPES_B_LOG2 = tuple(s * LOG2E for s in SLOPES_B)


def _moba_kernel(q_ref, k_ref, v_ref, o_ref, kmh_ref, kmm_ref, kml_ref, k2_ref, vt_ref, q2t_ref,
                 sel_ref, m_ref, acc_ref, ra_ref, rb_ref, pa_ref, pb_ref, aa_ref, ab_ref, *, nblk):
    hp = pl.program_id(1)
    tile = pl.program_id(2)
    BLK = MOBA_BLOCK
    NH = HEADS_PER_LANE_TILE
    PAIR = MOBA_PAIR
    QB = MOBA_QBLOCKS
    QT = QB * BLK
    ALIBI_LANE = (HEAD_DIM, 0)
    n0 = tile * QB
    n_past = n0 + QB - 1

    def lane_masks(rows):
        lane = lax.broadcasted_iota(jnp.int32, (rows, LANES), 1)
        in_head = [(lane >= hh * HEAD_DIM) & (lane < (hh + 1) * HEAD_DIM) for hh in range(NH)]
        alibi = [[lane == ALIBI_LANE[hh] + t for t in range(3)] for hh in range(NH)]
        return in_head, alibi

    slopes = []
    for hh in range(NH):
        slope = jnp.float32(SLOPES_B_LOG2[hh])
        for p in range(1, MOBA_HEADS // NH):
            slope = jnp.where(hp == p, jnp.float32(SLOPES_B_LOG2[NH * p + hh]), slope)
        slopes.append(slope)

    @pl.when(tile == 0)
    def _():
        in_head, alibi = lane_masks(BLK)
        key_off = lax.broadcasted_iota(jnp.int32, (BLK, LANES), 0).astype(F32)
        ones_tile = jnp.where(lax.broadcasted_iota(jnp.int32, (16, BLK), 0) == 0, 1.0, 0.0)
        cbs = []
        for hh in range(NH):
            terms = _split3(slopes[hh] * key_off)
            cb = jnp.zeros((BLK, LANES), F32)
            for t in range(3):
                cb = jnp.where(alibi[hh][t], terms[t].astype(F32), cb)
            cbs.append(cb)

        def body(j, c):
            rows = pl.ds(pl.multiple_of(j * BLK, BLK), BLK)
            kb = k_ref[0, rows, :].astype(F32)
            km = jnp.sum(kb, axis=0, keepdims=True) * (1.0 / BLK)
            hi, mid, lo = _split3(km)
            kmh_ref[pl.ds(j, 1), :] = hi.astype(F32)
            kmm_ref[pl.ds(j, 1), :] = mid.astype(F32)
            kml_ref[pl.ds(j, 1), :] = lo.astype(F32)
            vt = v_ref[0, rows, :].astype(F32).T
            for hh in range(NH):
                k2_ref[j, hh] = jnp.where(in_head[hh], kb, cbs[hh]).astype(BF16)
                vh = vt[hh * HEAD_DIM:(hh + 1) * HEAD_DIM, :]
                vt_ref[j, hh] = jnp.concatenate([vh, ones_tile], axis=0).astype(BF16)
            return c
        lax.fori_loop(0, nblk, body, 0)
        rb_ref[...] = jnp.zeros((PAIR, NH, BLK, QT), F32)
        pa_ref[...] = jnp.zeros((PAIR, NH, BLK, QT), BF16)
        aa_ref[...] = jnp.ones((PAIR, NH, 1, QT), F32)

    in_head_q, alibi_q = lane_masks(QT)
    qpos = lax.broadcasted_iota(jnp.int32, (1, NH * QT), 1)
    n_q = n0 + (qpos % QT) // BLK
    n_q_f = n_q.astype(F32)
    brow = lax.broadcasted_iota(jnp.int32, (nblk, NH * QT), 0)
    brow_f = brow.astype(F32)

    q_all = q_ref[0]
    kmh = kmh_ref[...].astype(BF16)
    kmm = kmm_ref[...].astype(BF16)
    kml = kml_ref[...].astype(BF16)

    qms = []
    for hh in range(NH):
        qm = jnp.where(in_head_q[hh], q_all, jnp.zeros_like(q_all))
        is_alibi = alibi_q[hh][0] | alibi_q[hh][1] | alibi_q[hh][2]
        q2t_ref[hh] = jnp.where(is_alibi, 1.0, qm.astype(F32)).T.astype(BF16)
        qms.append(qm)
    for hh in range(NH):
        for qb in range(QB):
            qs = slice(qb * BLK, (qb + 1) * BLK)
            ra_ref[0, hh, :, qs] = jnp.dot(k2_ref[n0 + qb, hh], q2t_ref[hh, :, qs],
                                           preferred_element_type=F32)
    gates = [_nt_dot(kmh, qm) + _nt_dot(kmm, qm) + _nt_dot(kml, qm) for qm in qms]
    g = jnp.where(brow < n_q, jnp.concatenate(gates, axis=1), NEG_INF)
    for t in range(MOBA_TOPK):
        mx = jnp.max(g, axis=0, keepdims=True)
        idx = jnp.min(jnp.where(g == mx, brow_f, float(nblk)), axis=0, keepdims=True)
        sel_ref[pl.ds(t, 1), :] = jnp.where(t < n_q, idx, -1.0)
        g = jnp.where(brow_f == idx, -jnp.inf, g)

    kk = lax.broadcasted_iota(jnp.int32, (BLK, QT), 0)
    tt = lax.broadcasted_iota(jnp.int32, (BLK, QT), 1) % BLK
    for hh in range(NH):
        r = jnp.where(kk <= tt, ra_ref[0, hh], NEG_INF)
        m = jnp.max(r, axis=0, keepdims=True)
        pb_ref[0, hh] = jnp.exp2(r - m).astype(BF16)
        m_ref[hh] = m
    for hh in range(NH):
        for qb in range(QB):
            qs = slice(qb * BLK, (qb + 1) * BLK)
            acc_ref[hh, :, qs] = jnp.dot(vt_ref[n0 + qb, hh], pb_ref[0, hh, :, qs],
                                         preferred_element_type=F32)

    last = n_past - 1

    def step(i, r_new, r_old, p_new, p_old, al_new, al_old):
        def scores():
            for k in range(PAIR):
                a = jnp.minimum(PAIR * i + k, last)
                for hh in range(NH):
                    r_new[k, hh] = jnp.dot(k2_ref[a, hh], q2t_ref[hh], preferred_element_type=F32)

        def pv():
            for hh in range(NH):
                acc = acc_ref[hh]
                for k in range(PAIR):
                    c = jnp.clip(PAIR * (i - 2) + k, 0, last)
                    acc = al_new[k, hh] * acc + jnp.dot(vt_ref[c, hh], p_new[k, hh],
                                                        preferred_element_type=F32)
                acc_ref[hh] = acc

        def softmax():
            for hh in range(NH):
                qs = slice(hh * QT, (hh + 1) * QT)
                m = m_ref[hh]
                for k in range(PAIR):
                    b = PAIR * (i - 1) + k
                    bf = jnp.where((b >= 0) & (b < n_past), b, -2).astype(F32)
                    r = r_old[k, hh]
                    chosen = ((sel_ref[0:1, qs] == bf) | (sel_ref[1:2, qs] == bf)
                              | (sel_ref[2:3, qs] == bf))
                    dist = (n_q_f[:, qs] - b.astype(F32)) * float(BLK)
                    off = jnp.where(chosen, -slopes[hh] * dist, NEG_INF)
                    m_new = jnp.maximum(m, jnp.max(r, axis=0, keepdims=True) + off)
                    al_old[k, hh] = jnp.exp2(m - m_new)
                    p_old[k, hh] = jnp.exp2(r - (m_new - off)).astype(BF16)
                    m = m_new
                m_ref[hh] = m

        for stage in (pv, softmax, scores):
            stage()

    def steps(first, count):
        for u in range(0, count, 2):
            step(first + u, ra_ref, rb_ref, pa_ref, pb_ref, aa_ref, ab_ref)
            step(first + u + 1, rb_ref, ra_ref, pb_ref, pa_ref, ab_ref, aa_ref)

    def body(t, carry):
        steps(MOBA_STEPS_PER_BODY * t, MOBA_STEPS_PER_BODY)
        return carry

    n_steps = (n_past + PAIR - 1) // PAIR + 2
    n_bodies = n_steps // MOBA_STEPS_PER_BODY
    lax.fori_loop(0, n_bodies, body, 0)
    rest = n_steps - n_bodies * MOBA_STEPS_PER_BODY
    for count in range(2, MOBA_STEPS_PER_BODY + 1, 2):
        @pl.when((rest > count - 2) & (rest <= count))
        def _(count=count):
            steps(n_bodies * MOBA_STEPS_PER_BODY, count)
    outs = []
    for hh in range(NH):
        a = acc_ref[hh]
        outs.append(a[:HEAD_DIM, :] / a[HEAD_DIM:HEAD_DIM + 1, :])
    o_ref[0] = jnp.concatenate(outs, axis=0).T.astype(BF16)


def _moba(proj, B, S):
    nblk = S // MOBA_BLOCK
    nh = HEADS_PER_LANE_TILE
    n_hp = MOBA_HEADS // nh
    qt = MOBA_QBLOCKS * MOBA_BLOCK
    cq, ck, cv = COL_BQ // LANES, COL_BK // LANES, COL_BV // LANES
    return pl.pallas_call(
        functools.partial(_moba_kernel, nblk=nblk),
        grid=(B, n_hp, S // qt),
        in_specs=[
            pl.BlockSpec((1, qt, LANES), lambda b, h, n: (b, n, cq + h)),
            pl.BlockSpec((1, S, LANES), lambda b, h, n: (b, 0, ck + h), pipeline_mode=pl.Buffered(1)),
            pl.BlockSpec((1, S, LANES), lambda b, h, n: (b, 0, cv + h), pipeline_mode=pl.Buffered(1)),
        ],
        out_specs=pl.BlockSpec((1, qt, LANES), lambda b, h, n: (b, n, h)),
        out_shape=jax.ShapeDtypeStruct((B, S, D_B), BF16),
        scratch_shapes=[pltpu.VMEM((nblk, LANES), F32)] * 3 + [
            pltpu.VMEM((nblk, nh, MOBA_BLOCK, LANES), BF16),
            pltpu.VMEM((nblk, nh, MOBA_VROWS, MOBA_BLOCK), BF16),
            pltpu.VMEM((nh, LANES, qt), BF16),
            pltpu.VMEM((8, nh * qt), F32),
            pltpu.VMEM((nh, 1, qt), F32),
            pltpu.VMEM((nh, MOBA_VROWS, qt), F32),
        ] + [pltpu.VMEM((MOBA_PAIR, nh, MOBA_BLOCK, qt), F32)] * 2
        + [pltpu.VMEM((MOBA_PAIR, nh, MOBA_BLOCK, qt), BF16)] * 2
        + [pltpu.VMEM((MOBA_PAIR, nh, 1, qt), F32)] * 2,
        compiler_params=pltpu.CompilerParams(
            dimension_semantics=("parallel", "parallel", "arbitrary"), vmem_limit_bytes=VMEM_LIMIT),
        name="moba",
    )(proj, proj, proj)


def _memkv_kernel(m_ref, nw_ref, w_ref, o_ref):
    h = _rmsnorm_bf16(m_ref[...], nw_ref[...])
    o_ref[...] = jnp.dot(h, w_ref[...], preferred_element_type=F32).astype(BF16)


def _memkv(mem2, mem_norm_w, w_bf16):
    R = mem2.shape[0]
    return pl.pallas_call(
        _memkv_kernel,
        out_shape=jax.ShapeDtypeStruct((R, 2 * D_M), BF16),
        compiler_params=pltpu.CompilerParams(vmem_limit_bytes=VMEM_LIMIT),
        name="memkv",
    )(mem2, mem_norm_w, w_bf16)


def _silu(x):
    return x * jax.nn.sigmoid(x)


def _merge_kernel(x_ref, o1_ref, o2_ref, o3_ref, l1_ref, l2_ref, l3_ref, ob_ref, ga_ref, gb_ref,
                  mq_ref, mg_ref, lg_ref, kv_ref, bm_ref, wa_ref, wb_ref, wm_ref, wo_ref, fw_ref,
                  out_ref):
    tm = x_ref.shape[1]

    perms = {}

    def perm_matrix(d):
        if d not in perms:
            t = lax.broadcasted_iota(jnp.int32, (tm, tm), 0)
            src = lax.broadcasted_iota(jnp.int32, (tm, tm), 1)
            perms[d] = jnp.where((t % d) * (tm // d) + t // d == src, 1.0, 0.0).astype(BF16)
        return perms[d]

    def natural(ref, width, exact_f32):
        d = ref.shape[2] // width
        if d == 1:
            return ref[0].astype(F32)
        stacked = jnp.concatenate([ref[0, :, r * width:(r + 1) * width] for r in range(d)], axis=0)
        perm = perm_matrix(d)
        if not exact_f32:
            return jnp.dot(perm, stacked, preferred_element_type=F32)
        hi, mid, _ = _split3(stacked)
        return (jnp.dot(perm, hi, preferred_element_type=F32)
                + jnp.dot(perm, mid, preferred_element_type=F32))

    l1, l2, l3 = (natural(r, LANES, True) for r in (l1_ref, l2_ref, l3_ref))
    lmax = jnp.maximum(jnp.maximum(l1, l2), l3)
    e1, e2, e3 = jnp.exp(l1 - lmax), jnp.exp(l2 - lmax), jnp.exp(l3 - lmax)
    inv = 1.0 / (e1 + e2 + e3)
    erow = lax.broadcasted_iota(jnp.int32, (LANES, D_A), 0)
    ecol = lax.broadcasted_iota(jnp.int32, (LANES, D_A), 1)
    expand = jnp.where((ecol >= erow * HEAD_DIM) & (ecol < (erow + 1) * HEAD_DIM), 1.0, 0.0).astype(BF16)

    def per_head(w):
        hi, mid, _ = _split3(w)
        return (jnp.dot(hi, expand, preferred_element_type=F32)
                + jnp.dot(mid, expand, preferred_element_type=F32))

    def merge_gate(k):
        cs = slice(k * D_MODEL, (k + 1) * D_MODEL)
        return jax.nn.sigmoid(lg_ref[0, :, cs].astype(F32) + bm_ref[:, cs])

    o_a = (per_head(e1 * inv) * natural(o1_ref, D_A, False)
           + per_head(e2 * inv) * natural(o2_ref, D_A, False)
           + per_head(e3 * inv) * natural(o3_ref, D_A, False))
    t_a = (o_a * _silu(ga_ref[0].astype(F32))).astype(BF16)
    merged = merge_gate(0) * jnp.dot(t_a, wa_ref[...], preferred_element_type=F32)

    t_b = (ob_ref[0].astype(F32) * _silu(gb_ref[0].astype(F32))).astype(BF16)
    merged = merged + merge_gate(1) * jnp.dot(t_b, wb_ref[...], preferred_element_type=F32)

    mq = mq_ref[0] * ATTN_SCALE
    km = kv_ref[0, :, :D_M]
    vm = kv_ref[0, :, D_M:]
    mlane = lax.broadcasted_iota(jnp.int32, (tm, D_M), 1)
    o_m = jnp.zeros((tm, D_M), F32)
    for h in range(MEM_HEADS):
        in_head = (mlane >= h * HEAD_DIM) & (mlane < (h + 1) * HEAD_DIM)
        s = _nt_dot(jnp.where(in_head, mq, jnp.zeros_like(mq)), km)
        s = s - jnp.max(s, axis=-1, keepdims=True)
        e = jnp.exp(s)
        p = (e * (1.0 / jnp.sum(e, axis=-1, keepdims=True))).astype(BF16)
        o_m = jnp.where(in_head, jnp.dot(p, vm, preferred_element_type=F32), o_m)
    t_m = (o_m * _silu(mg_ref[0].astype(F32))).astype(BF16)
    merged = merged + merge_gate(2) * jnp.dot(t_m, wm_ref[...], preferred_element_type=F32)

    y = x_ref[0] + jnp.dot(merged.astype(BF16), wo_ref[...], preferred_element_type=F32)
    ms = jnp.mean(y * y, axis=-1, keepdims=True)
    out_ref[0] = y * lax.rsqrt(ms + EPS) * fw_ref[...]


def _merge(x, o_groups, lse_groups, o_b, proj, kv, b_merge, wa, wb, wm, wo, fw, tm=256):
    B, S, _ = x.shape

    def tok(width, colblk):
        return pl.BlockSpec((1, tm, width), lambda b, i: (b, i, colblk))

    def dil(width, d):
        return pl.BlockSpec((1, tm // d, d * width), lambda b, i: (b, i, 0))

    def full(shape):
        return pl.BlockSpec(shape, lambda b, i: (0,) * len(shape))

    in_specs = (
        [tok(D_MODEL, 0)]
        + [dil(D_A, d) for _, d in DIL_PAIRS]
        + [dil(LANES, d) for _, d in DIL_PAIRS]
        + [tok(D_B, 0)]
        + [tok(D_A, COL_A_GATE // D_A), tok(D_B, COL_B_GATE // D_B),
           tok(D_M, COL_MQ // D_M), tok(D_M, COL_M_GATE // D_M), tok(D_MERGE, COL_MERGE // D_MERGE)]
        + [pl.BlockSpec((1, kv.shape[1], 2 * D_M), lambda b, i: (b, 0, 0))]
        + [full((1, D_MERGE)), full((D_A, D_MODEL)), full((D_B, D_MODEL)), full((D_M, D_MODEL)),
           full((D_MODEL, D_MODEL)), full((1, D_MODEL))]
    )
    return pl.pallas_call(
        _merge_kernel,
        grid=(B, S // tm),
        in_specs=in_specs,
        out_specs=tok(D_MODEL, 0),
        out_shape=jax.ShapeDtypeStruct((B, S, D_MODEL), F32),
        compiler_params=pltpu.CompilerParams(
            dimension_semantics=("parallel", "parallel"), vmem_limit_bytes=VMEM_LIMIT),
        name="merge",
    )(x, *o_groups, *lse_groups, o_b, proj, proj, proj, proj, proj, kv, b_merge, wa, wb, wm, wo, fw)


def kernel(x, mem, norm_w, mem_norm_w, w_in, b_merge, w_mem_kv, w_branch_a, w_branch_b,
           w_branch_m, w_out, final_norm_w):
    B, S, D = x.shape
    assert w_in.shape[0] == 1, "single-layer block"
    w = w_in[0]
    nw = norm_w[0][None, :]
    src_a_gate = N_GROUPS * D_QKV
    src_bq = src_a_gate + D_A
    src_bk = src_bq + D_B
    src_merge = src_bk + 3 * D_B + 2 * D_M
    assert src_a_gate % D_QKV == 0 and src_merge % D_QKV == 0 and w.shape[1] % D_QKV == 0
    col = jnp.arange(w.shape[1])
    col_scale = jnp.where((col >= src_bq) & (col < src_bk), ATTN_SCALE * LOG2E, 1.0).astype(F32)
    w_bf16 = (w * col_scale[None, :]).astype(BF16)
    tiles_rest = list(range(src_a_gate // D_QKV, src_merge // D_QKV))
    tiles_merge = list(range(src_merge // D_QKV, w.shape[1] // D_QKV))
    proj = _inproj(x.reshape(B * S, D), nw, w_bf16, "inproj",
                   tiles_merge + [0] + tiles_rest).reshape(B, S, D_MAIN)

    o_groups, lse_groups = [], []
    for g, (_, dilation) in enumerate(DIL_PAIRS):
        if dilation == 1:
            qkv, col0 = proj.reshape(B, 1, S, D_MAIN), COL_A0 // D_A
        else:
            x_g = x.reshape(B, S // dilation, dilation, D).transpose(0, 2, 1, 3).reshape(B * S, D)
            qkv = _inproj(x_g, nw, w_bf16, f"inproj_d{dilation}", [g])
            qkv = qkv.reshape(B, dilation, S // dilation, D_QKV)
            col0 = 0
        o_g, lse_g = _dilated_group(qkv, col0, g, B, S)
        o_groups.append(o_g)
        lse_groups.append(lse_g)
    o_b = _moba(proj, B, S)
    kv = _memkv(mem.reshape(-1, D), mem_norm_w[0][None, :], w_mem_kv[0].astype(BF16))
    kv = kv.reshape(B, mem.shape[1], 2 * D_M)
    return _merge(x, o_groups, lse_groups, o_b, proj, kv, b_merge[0][None, :],
                  w_branch_a[0].astype(BF16), w_branch_b[0].astype(BF16),
                  w_branch_m[0].astype(BF16), w_out[0].astype(BF16), final_norm_w[None, :])
```

```python
import functools
import math

import jax
import jax.numpy as jnp
from jax import lax
from jax.experimental import pallas as pl
from jax.experimental.pallas import tpu as pltpu

F32 = jnp.float32
BF16 = jnp.bfloat16

D_MODEL = 1024
HEAD_DIM = 64
ATTN_SCALE = HEAD_DIM ** -0.5
LOG2E = math.log2(math.e)
DIL_PAIRS = ((128, 1), (512, 4), (2048, 16))
N_GROUPS = len(DIL_PAIRS)
DIL_HEADS = 8
DIL_QBLK = 128
DIL_QBLOCKS = 8
MOBA_HEADS = 8
MOBA_BLOCK = 256
MOBA_TOPK = 3
MEM_HEADS = 4
N_BRANCHES = 3
EPS = 1e-6
NEG_INF = -1e30

D_A = DIL_HEADS * HEAD_DIM
D_B = MOBA_HEADS * HEAD_DIM
D_M = MEM_HEADS * HEAD_DIM
D_MERGE = N_BRANCHES * D_MODEL
D_QKV = 3 * D_A

COL_MERGE = 0
COL_A0 = D_MERGE
COL_A_GATE = COL_A0 + D_QKV
COL_BQ = COL_A_GATE + D_A
COL_BK = COL_BQ + D_B
COL_BV = COL_BK + D_B
COL_B_GATE = COL_BV + D_B
COL_MQ = COL_B_GATE + D_B
COL_M_GATE = COL_MQ + D_M
D_MAIN = COL_M_GATE + D_M

LANES = 128
HEADS_PER_LANE_TILE = LANES // HEAD_DIM

SLOPES_A = tuple(2.0 ** (-8.0 * k / 16) for k in range(1, 17, 2))
SLOPES_B = tuple(2.0 ** (-8.0 * k / 16) for k in range(2, 17, 2))

VMEM_LIMIT = 48 * 1024 * 1024


def _nt_dot(a, b):
    return lax.dot_general(a, b, (((1,), (1,)), ((), ())), preferred_element_type=F32)


def _split3(v):
    hi = v.astype(BF16)
    r1 = v - hi.astype(F32)
    mid = r1.astype(BF16)
    lo = (r1 - mid.astype(F32)).astype(BF16)
    return hi, mid, lo


def _rmsnorm_bf16(xf, w):
    ms = jnp.mean(xf * xf, axis=-1, keepdims=True)
    return (xf * lax.rsqrt(ms + EPS) * w).astype(BF16)


def _inproj_kernel(x_ref, nw_ref, w_ref, o_ref, h_ref):
    @pl.when(pl.program_id(1) == 0)
    def _():
        h_ref[...] = _rmsnorm_bf16(x_ref[...], nw_ref[...])

    o_ref[...] = jnp.dot(h_ref[...], w_ref[...], preferred_element_type=F32).astype(BF16)


def _inproj(x2, norm_w, w_bf16, name, col_tiles, tm=1024):
    T = x2.shape[0]
    tn = D_QKV
    n_cols = len(col_tiles) * tn

    def w_tile(j):
        t = col_tiles[0]
        for k, c in enumerate(col_tiles[1:], 1):
            t = jnp.where(j == k, c, t)
        return t

    return pl.pallas_call(
        _inproj_kernel,
        grid=(T // tm, len(col_tiles)),
        in_specs=[
            pl.BlockSpec((tm, D_MODEL), lambda i, j: (i, 0)),
            pl.BlockSpec((1, D_MODEL), lambda i, j: (0, 0)),
            pl.BlockSpec((D_MODEL, tn), lambda i, j: (0, w_tile(j))),
        ],
        out_specs=pl.BlockSpec((tm, tn), lambda i, j: (i, j)),
        out_shape=jax.ShapeDtypeStruct((T, n_cols), BF16),
        scratch_shapes=[pltpu.VMEM((tm, D_MODEL), BF16)],
        compiler_params=pltpu.CompilerParams(
            dimension_semantics=("parallel", "arbitrary"), vmem_limit_bytes=VMEM_LIMIT),
        name=name,
    )(x2, norm_w, w_bf16)


def _dilated_kernel(q_ref, kp_ref, kc_ref, vp_ref, vc_ref, o_ref, lse_ref, bias_ref, s_ref, p_ref,
                    *, dilation, span):
    blk = pl.program_id(2)
    Q = DIL_QBLK
    NH = HEADS_PER_LANE_TILE
    n_tiles = DIL_HEADS // NH

    @pl.when((pl.program_id(0) == 0) & (pl.program_id(1) == 0) & (blk == 0))
    def _():
        row = lax.broadcasted_iota(jnp.int32, (Q, 2 * Q), 0)
        col = lax.broadcasted_iota(jnp.int32, (Q, 2 * Q), 1)
        steps = row + Q - col
        in_band = (steps >= 0) & (steps <= span)
        dist = (steps * dilation).astype(F32)
        for h in range(DIL_HEADS):
            bias = jnp.where(in_band, -SLOPES_A[h] * dist, NEG_INF)
            bias_ref[0, h] = bias
            bias_ref[1, h] = jnp.where(col >= Q, bias, NEG_INF)

    lane = lax.broadcasted_iota(jnp.int32, (Q, LANES), 1)
    in_head = [(lane >= hh * HEAD_DIM) & (lane < (hh + 1) * HEAD_DIM) for hh in range(NH)]

    for qb in range(DIL_QBLOCKS):
        rows = slice(qb * Q, (qb + 1) * Q)
        variant = (blk == 0).astype(jnp.int32) if qb == 0 else 0

        def band(prev_ref, cur_ref, ls):
            if qb == 0:
                return jnp.concatenate([prev_ref[0, 0, :, ls], cur_ref[0, 0, :Q, ls]], axis=0)
            return cur_ref[0, 0, (qb - 1) * Q:(qb + 1) * Q, ls]

        for t in range(n_tiles):
            ls = slice(t * LANES, (t + 1) * LANES)
            q = q_ref[0, 0, rows, ls] * ATTN_SCALE
            k = band(kp_ref, kc_ref, ls)
            for hh in range(NH):
                s_ref[NH * t + hh] = _nt_dot(jnp.where(in_head[hh], q, jnp.zeros_like(q)), k)

        lse_tile = jnp.zeros((Q, LANES), F32)
        inv_den = []
        for h in range(DIL_HEADS):
            s = s_ref[h] + bias_ref[variant, h]
            m = jnp.max(s, axis=-1, keepdims=True)
            e = jnp.exp(s - m)
            den = jnp.sum(e, axis=-1, keepdims=True)
            p_ref[h] = e.astype(BF16)
            inv_den.append(1.0 / den)
            lse_tile = jnp.where(lane == h, m + jnp.log(den), lse_tile)
        lse_ref[0, rows, :] = lse_tile

        for t in range(n_tiles):
            ls = slice(t * LANES, (t + 1) * LANES)
            v = band(vp_ref, vc_ref, ls)
            o = [jnp.dot(p_ref[NH * t + hh], v, preferred_element_type=F32) * inv_den[NH * t + hh]
                 for hh in range(NH)]
            o_ref[0, rows, ls] = jnp.where(in_head[0], o[0], o[1]).astype(BF16)


def _dilated_group(qkv, col0, g, B, S):
    window, dilation = DIL_PAIRS[g]
    n = S // dilation
    nb = n // DIL_QBLK
    span = window // dilation

    rows = DIL_QBLOCKS * DIL_QBLK

    def spec(piece, prev):
        if prev:
            return pl.BlockSpec((1, 1, DIL_QBLK, D_A),
                                lambda b, r, i: (b, r, jnp.maximum(DIL_QBLOCKS * i - 1, 0), col0 + piece))
        return pl.BlockSpec((1, 1, rows, D_A), lambda b, r, i: (b, r, i, col0 + piece))

    o, lse = pl.pallas_call(
        functools.partial(_dilated_kernel, dilation=dilation, span=span),
        grid=(B, dilation, nb // DIL_QBLOCKS),
        in_specs=[spec(0, False), spec(1, True), spec(1, False), spec(2, True), spec(2, False)],
        out_specs=[
            pl.BlockSpec((1, rows, D_A), lambda b, r, i: (b, i, r)),
            pl.BlockSpec((1, rows, LANES), lambda b, r, i: (b, i, r)),
        ],
        out_shape=[
            jax.ShapeDtypeStruct((B, n, dilation * D_A), BF16),
            jax.ShapeDtypeStruct((B, n, dilation * LANES), F32),
        ],
        scratch_shapes=[
            pltpu.VMEM((2, DIL_HEADS, DIL_QBLK, 2 * DIL_QBLK), F32),
            pltpu.VMEM((DIL_HEADS, DIL_QBLK, 2 * DIL_QBLK), F32),
            pltpu.VMEM((DIL_HEADS, DIL_QBLK, 2 * DIL_QBLK), BF16),
        ],
        compiler_params=pltpu.CompilerParams(
            dimension_semantics=("arbitrary", "arbitrary", "arbitrary"), vmem_limit_bytes=VMEM_LIMIT),
        name=f"dilated_g{g}",
    )(qkv, qkv, qkv, qkv, qkv)
    return o, lse


MOBA_PAIR = 1
MOBA_QBLOCKS = 2
MOBA_STEPS_PER_BODY = 8
MOBA_VROWS = HEAD_DIM + 16
SLOPES_B_LOG2 = tuple(s * LOG2E for s in SLOPES_B)


def _moba_kernel(q_ref, k_ref, v_ref, o_ref, kmh_ref, kmm_ref, kml_ref, k2_ref, vt_ref, q2t_ref,
                 sel_ref, m_ref, acc_ref, ra_ref, rb_ref, pa_ref, pb_ref, aa_ref, ab_ref, *, nblk):
    hp = pl.program_id(1)
    tile = pl.program_id(2)
    BLK = MOBA_BLOCK
    NH = HEADS_PER_LANE_TILE
    PAIR = MOBA_PAIR
    QB = MOBA_QBLOCKS
    QT = QB * BLK
    ALIBI_LANE = (HEAD_DIM, 0)
    n0 = tile * QB
    n_past = n0 + QB - 1

    def lane_masks(rows):
        lane = lax.broadcasted_iota(jnp.int32, (rows, LANES), 1)
        in_head = [(lane >= hh * HEAD_DIM) & (lane < (hh + 1) * HEAD_DIM) for hh in range(NH)]
        alibi = [[lane == ALIBI_LANE[hh] + t for t in range(3)] for hh in range(NH)]
        return in_head, alibi

    slopes = []
    for hh in range(NH):
        slope = jnp.float32(SLOPES_B_LOG2[hh])
        for p in range(1, MOBA_HEADS // NH):
            slope = jnp.where(hp == p, jnp.float32(SLOPES_B_LOG2[NH * p + hh]), slope)
        slopes.append(slope)

    @pl.when(tile == 0)
    def _():
        in_head, alibi = lane_masks(BLK)
        key_off = lax.broadcasted_iota(jnp.int32, (BLK, LANES), 0).astype(F32)
        ones_tile = jnp.where(lax.broadcasted_iota(jnp.int32, (16, BLK), 0) == 0, 1.0, 0.0)
        cbs = []
        for hh in range(NH):
            terms = _split3(slopes[hh] * key_off)
            cb = jnp.zeros((BLK, LANES), F32)
            for t in range(3):
                cb = jnp.where(alibi[hh][t], terms[t].astype(F32), cb)
            cbs.append(cb)

        def body(j, c):
            rows = pl.ds(pl.multiple_of(j * BLK, BLK), BLK)
            kb = k_ref[0, rows, :].astype(F32)
            km = jnp.sum(kb, axis=0, keepdims=True) * (1.0 / BLK)
            hi, mid, lo = _split3(km)
            kmh_ref[pl.ds(j, 1), :] = hi.astype(F32)
            kmm_ref[pl.ds(j, 1), :] = mid.astype(F32)
            kml_ref[pl.ds(j, 1), :] = lo.astype(F32)
            vt = v_ref[0, rows, :].astype(F32).T
            for hh in range(NH):
                k2_ref[j, hh] = jnp.where(in_head[hh], kb, cbs[hh]).astype(BF16)
                vh = vt[hh * HEAD_DIM:(hh + 1) * HEAD_DIM, :]
                vt_ref[j, hh] = jnp.concatenate([vh, ones_tile], axis=0).astype(BF16)
            return c
        lax.fori_loop(0, nblk, body, 0)
        rb_ref[...] = jnp.zeros((PAIR, NH, BLK, QT), F32)
        pa_ref[...] = jnp.zeros((PAIR, NH, BLK, QT), BF16)
        aa_ref[...] = jnp.ones((PAIR, NH, 1, QT), F32)

    in_head_q, alibi_q = lane_masks(QT)
    qpos = lax.broadcasted_iota(jnp.int32, (1, NH * QT), 1)
    n_q = n0 + (qpos % QT) // BLK
    n_q_f = n_q.astype(F32)
    brow = lax.broadcasted_iota(jnp.int32, (nblk, NH * QT), 0)
    brow_f = brow.astype(F32)

    q_all = q_ref[0]
    kmh = kmh_ref[...].astype(BF16)
    kmm = kmm_ref[...].astype(BF16)
    kml = kml_ref[...].astype(BF16)

    qms = []
    for hh in range(NH):
        qm = jnp.where(in_head_q[hh], q_all, jnp.zeros_like(q_all))
        is_alibi = alibi_q[hh][0] | alibi_q[hh][1] | alibi_q[hh][2]
        q2t_ref[hh] = jnp.where(is_alibi, 1.0, qm.astype(F32)).T.astype(BF16)
        qms.append(qm)
    for hh in range(NH):
        for qb in range(QB):
            qs = slice(qb * BLK, (qb + 1) * BLK)
            ra_ref[0, hh, :, qs] = jnp.dot(k2_ref[n0 + qb, hh], q2t_ref[hh, :, qs],
                                           preferred_element_type=F32)
    gates = [_nt_dot(kmh, qm) + _nt_dot(kmm, qm) + _nt_dot(kml, qm) for qm in qms]
    g = jnp.where(brow < n_q, jnp.concatenate(gates, axis=1), NEG_INF)
    for t in range(MOBA_TOPK):
        mx = jnp.max(g, axis=0, keepdims=True)
        idx = jnp.min(jnp.where(g == mx, brow_f, float(nblk)), axis=0, keepdims=True)
        sel_ref[pl.ds(t, 1), :] = jnp.where(t < n_q, idx, -1.0)
        g = jnp.where(brow_f == idx, -jnp.inf, g)

    kk = lax.broadcasted_iota(jnp.int32, (BLK, QT), 0)
    tt = lax.broadcasted_iota(jnp.int32, (BLK, QT), 1) % BLK
    for hh in range(NH):
        r = jnp.where(kk <= tt, ra_ref[0, hh], NEG_INF)
        m = jnp.max(r, axis=0, keepdims=True)
        pb_ref[0, hh] = jnp.exp2(r - m).astype(BF16)
        m_ref[hh] = m
    for hh in range(NH):
        for qb in range(QB):
            qs = slice(qb * BLK, (qb + 1) * BLK)
            acc_ref[hh, :, qs] = jnp.dot(vt_ref[n0 + qb, hh], pb_ref[0, hh, :, qs],
                                         preferred_element_type=F32)

    last = n_past - 1

    def step(i, r_new, r_old, p_new, p_old, al_new, al_old):
        def scores():
            for k in range(PAIR):
                a = jnp.minimum(PAIR * i + k, last)
                for hh in range(NH):
                    r_new[k, hh] = jnp.dot(k2_ref[a, hh], q2t_ref[hh], preferred_element_type=F32)

        def pv():
            for hh in range(NH):
                acc = acc_ref[hh]
                for k in range(PAIR):
                    c = jnp.clip(PAIR * (i - 2) + k, 0, last)
                    acc = al_new[k, hh] * acc + jnp.dot(vt_ref[c, hh], p_new[k, hh],
                                                        preferred_element_type=F32)
                acc_ref[hh] = acc

        def softmax():
            for hh in range(NH):
                qs = slice(hh * QT, (hh + 1) * QT)
                m = m_ref[hh]
                for k in range(PAIR):
                    b = PAIR * (i - 1) + k
                    bf = jnp.where((b >= 0) & (b < n_past), b, -2).astype(F32)
                    r = r_old[k, hh]
                    chosen = ((sel_ref[0:1, qs] == bf) | (sel_ref[1:2, qs] == bf)
                              | (sel_ref[2:3, qs] == bf))
                    dist = (n_q_f[:, qs] - b.astype(F32)) * float(BLK)
                    off = jnp.where(chosen, -slopes[hh] * dist, NEG_INF)
                    m_new = jnp.maximum(m, jnp.max(r, axis=0, keepdims=True) + off)
                    al_old[k, hh] = jnp.exp2(m - m_new)
                    p_old[k, hh] = jnp.exp2(r - (m_new - off)).astype(BF16)
                    m = m_new
                m_ref[hh] = m

        for stage in (pv, softmax, scores):
            stage()

    def steps(first, count):
        for u in range(0, count, 2):
            step(first + u, ra_ref, rb_ref, pa_ref, pb_ref, aa_ref, ab_ref)
            step(first + u + 1, rb_ref, ra_ref, pb_ref, pa_ref, ab_ref, aa_ref)

    def body(t, carry):
        steps(MOBA_STEPS_PER_BODY * t, MOBA_STEPS_PER_BODY)
        return carry

    n_steps = (n_past + PAIR - 1) // PAIR + 2
    n_bodies = n_steps // MOBA_STEPS_PER_BODY
    lax.fori_loop(0, n_bodies, body, 0)
    rest = n_steps - n_bodies * MOBA_STEPS_PER_BODY
    for count in range(2, MOBA_STEPS_PER_BODY + 1, 2):
        @pl.when((rest > count - 2) & (rest <= count))
        def _(count=count):
            steps(n_bodies * MOBA_STEPS_PER_BODY, count)
    outs = []
    for hh in range(NH):
        a = acc_ref[hh]
        outs.append(a[:HEAD_DIM, :] / a[HEAD_DIM:HEAD_DIM + 1, :])
    o_ref[0] = jnp.concatenate(outs, axis=0).T.astype(BF16)


def _moba(proj, B, S):
    nblk = S // MOBA_BLOCK
    nh = HEADS_PER_LANE_TILE
    n_hp = MOBA_HEADS // nh
    qt = MOBA_QBLOCKS * MOBA_BLOCK
    cq, ck, cv = COL_BQ // LANES, COL_BK // LANES, COL_BV // LANES
    return pl.pallas_call(
        functools.partial(_moba_kernel, nblk=nblk),
        grid=(B, n_hp, S // qt),
        in_specs=[
            pl.BlockSpec((1, qt, LANES), lambda b, h, n: (b, n, cq + h)),
            pl.BlockSpec((1, S, LANES), lambda b, h, n: (b, 0, ck + h), pipeline_mode=pl.Buffered(1)),
            pl.BlockSpec((1, S, LANES), lambda b, h, n: (b, 0, cv + h), pipeline_mode=pl.Buffered(1)),
        ],
        out_specs=pl.BlockSpec((1, qt, LANES), lambda b, h, n: (b, n, h)),
        out_shape=jax.ShapeDtypeStruct((B, S, D_B), BF16),
        scratch_shapes=[pltpu.VMEM((nblk, LANES), F32)] * 3 + [
            pltpu.VMEM((nblk, nh, MOBA_BLOCK, LANES), BF16),
            pltpu.VMEM((nblk, nh, MOBA_VROWS, MOBA_BLOCK), BF16),
            pltpu.VMEM((nh, LANES, qt), BF16),
            pltpu.VMEM((8, nh * qt), F32),
            pltpu.VMEM((nh, 1, qt), F32),
            pltpu.VMEM((nh, MOBA_VROWS, qt), F32),
        ] + [pltpu.VMEM((MOBA_PAIR, nh, MOBA_BLOCK, qt), F32)] * 2
        + [pltpu.VMEM((MOBA_PAIR, nh, MOBA_BLOCK, qt), BF16)] * 2
        + [pltpu.VMEM((MOBA_PAIR, nh, 1, qt), F32)] * 2,
        compiler_params=pltpu.CompilerParams(
            dimension_semantics=("parallel", "parallel", "arbitrary"), vmem_limit_bytes=VMEM_LIMIT),
        name="moba",
    )(proj, proj, proj)


def _memkv_kernel(m_ref, nw_ref, w_ref, o_ref):
    h = _rmsnorm_bf16(m_ref[...], nw_ref[...])
    o_ref[...] = jnp.dot(h, w_ref[...], preferred_element_type=F32).astype(BF16)


def _memkv(mem2, mem_norm_w, w_bf16):
    R = mem2.shape[0]
    return pl.pallas_call(
        _memkv_kernel,
        out_shape=jax.ShapeDtypeStruct((R, 2 * D_M), BF16),
        compiler_params=pltpu.CompilerParams(vmem_limit_bytes=VMEM_LIMIT),
        name="memkv",
    )(mem2, mem_norm_w, w_bf16)


def _silu(x):
    return x * jax.nn.sigmoid(x)


def _merge_kernel(x_ref, o1_ref, o2_ref, o3_ref, l1_ref, l2_ref, l3_ref, ob_ref, ga_ref, gb_ref,
                  mq_ref, mg_ref, lg_ref, kv_ref, bm_ref, wa_ref, wb_ref, wm_ref, wo_ref, fw_ref,
                  out_ref):
    tm = x_ref.shape[1]

    perms = {}

    def perm_matrix(d):
        if d not in perms:
            t = lax.broadcasted_iota(jnp.int32, (tm, tm), 0)
            src = lax.broadcasted_iota(jnp.int32, (tm, tm), 1)
            perms[d] = jnp.where((t % d) * (tm // d) + t // d == src, 1.0, 0.0).astype(BF16)
        return perms[d]

    def natural(ref, width, exact_f32):
        d = ref.shape[2] // width
        if d == 1:
            return ref[0].astype(F32)
        stacked = jnp.concatenate([ref[0, :, r * width:(r + 1) * width] for r in range(d)], axis=0)
        perm = perm_matrix(d)
        if not exact_f32:
            return jnp.dot(perm, stacked, preferred_element_type=F32)
        hi, mid, _ = _split3(stacked)
        return (jnp.dot(perm, hi, preferred_element_type=F32)
                + jnp.dot(perm, mid, preferred_element_type=F32))

    l1, l2, l3 = (natural(r, LANES, True) for r in (l1_ref, l2_ref, l3_ref))
    lmax = jnp.maximum(jnp.maximum(l1, l2), l3)
    e1, e2, e3 = jnp.exp(l1 - lmax), jnp.exp(l2 - lmax), jnp.exp(l3 - lmax)
    inv = 1.0 / (e1 + e2 + e3)
    erow = lax.broadcasted_iota(jnp.int32, (LANES, D_A), 0)
    ecol = lax.broadcasted_iota(jnp.int32, (LANES, D_A), 1)
    expand = jnp.where((ecol >= erow * HEAD_DIM) & (ecol < (erow + 1) * HEAD_DIM), 1.0, 0.0).astype(BF16)

    def per_head(w):
        hi, mid, _ = _split3(w)
        return (jnp.dot(hi, expand, preferred_element_type=F32)
                + jnp.dot(mid, expand, preferred_element_type=F32))

    def merge_gate(k):
        cs = slice(k * D_MODEL, (k + 1) * D_MODEL)
        return jax.nn.sigmoid(lg_ref[0, :, cs].astype(F32) + bm_ref[:, cs])

    o_a = (per_head(e1 * inv) * natural(o1_ref, D_A, False)
           + per_head(e2 * inv) * natural(o2_ref, D_A, False)
           + per_head(e3 * inv) * natural(o3_ref, D_A, False))
    t_a = (o_a * _silu(ga_ref[0].astype(F32))).astype(BF16)
    merged = merge_gate(0) * jnp.dot(t_a, wa_ref[...], preferred_element_type=F32)

    t_b = (ob_ref[0].astype(F32) * _silu(gb_ref[0].astype(F32))).astype(BF16)
    merged = merged + merge_gate(1) * jnp.dot(t_b, wb_ref[...], preferred_element_type=F32)

    mq = mq_ref[0] * ATTN_SCALE
    km = kv_ref[0, :, :D_M]
    vm = kv_ref[0, :, D_M:]
    mlane = lax.broadcasted_iota(jnp.int32, (tm, D_M), 1)
    o_m = jnp.zeros((tm, D_M), F32)
    for h in range(MEM_HEADS):
        in_head = (mlane >= h * HEAD_DIM) & (mlane < (h + 1) * HEAD_DIM)
        s = _nt_dot(jnp.where(in_head, mq, jnp.zeros_like(mq)), km)
        s = s - jnp.max(s, axis=-1, keepdims=True)
        e = jnp.exp(s)
        p = (e * (1.0 / jnp.sum(e, axis=-1, keepdims=True))).astype(BF16)
        o_m = jnp.where(in_head, jnp.dot(p, vm, preferred_element_type=F32), o_m)
    t_m = (o_m * _silu(mg_ref[0].astype(F32))).astype(BF16)
    merged = merged + merge_gate(2) * jnp.dot(t_m, wm_ref[...], preferred_element_type=F32)

    y = x_ref[0] + jnp.dot(merged.astype(BF16), wo_ref[...], preferred_element_type=F32)
    ms = jnp.mean(y * y, axis=-1, keepdims=True)
    out_ref[0] = y * lax.rsqrt(ms + EPS) * fw_ref[...]


def _merge(x, o_groups, lse_groups, o_b, proj, kv, b_merge, wa, wb, wm, wo, fw, tm=256):
    B, S, _ = x.shape

    def tok(width, colblk):
        return pl.BlockSpec((1, tm, width), lambda b, i: (b, i, colblk))

    def dil(width, d):
        return pl.BlockSpec((1, tm // d, d * width), lambda b, i: (b, i, 0))

    def full(shape):
        return pl.BlockSpec(shape, lambda b, i: (0,) * len(shape))

    in_specs = (
        [tok(D_MODEL, 0)]
        + [dil(D_A, d) for _, d in DIL_PAIRS]
        + [dil(LANES, d) for _, d in DIL_PAIRS]
        + [tok(D_B, 0)]
        + [tok(D_A, COL_A_GATE // D_A), tok(D_B, COL_B_GATE // D_B),
           tok(D_M, COL_MQ // D_M), tok(D_M, COL_M_GATE // D_M), tok(D_MERGE, COL_MERGE // D_MERGE)]
        + [pl.BlockSpec((1, kv.shape[1], 2 * D_M), lambda b, i: (b, 0, 0))]
        + [full((1, D_MERGE)), full((D_A, D_MODEL)), full((D_B, D_MODEL)), full((D_M, D_MODEL)),
           full((D_MODEL, D_MODEL)), full((1, D_MODEL))]
    )
    return pl.pallas_call(
        _merge_kernel,
        grid=(B, S // tm),
        in_specs=in_specs,
        out_specs=tok(D_MODEL, 0),
        out_shape=jax.ShapeDtypeStruct((B, S, D_MODEL), F32),
        compiler_params=pltpu.CompilerParams(
            dimension_semantics=("parallel", "parallel"), vmem_limit_bytes=VMEM_LIMIT),
        name="merge",
    )(x, *o_groups, *lse_groups, o_b, proj, proj, proj, proj, proj, kv, b_merge, wa, wb, wm, wo, fw)


def kernel(x, mem, norm_w, mem_norm_w, w_in, b_merge, w_mem_kv, w_branch_a, w_branch_b,
           w_branch_m, w_out, final_norm_w):
    B, S, D = x.shape
    assert w_in.shape[0] == 1, "single-layer block"
    w = w_in[0]
    nw = norm_w[0][None, :]
    src_a_gate = N_GROUPS * D_QKV
    src_bq = src_a_gate + D_A
    src_bk = src_bq + D_B
    src_merge = src_bk + 3 * D_B + 2 * D_M
    assert src_a_gate % D_QKV == 0 and src_merge % D_QKV == 0 and w.shape[1] % D_QKV == 0
    col = jnp.arange(w.shape[1])
    col_scale = jnp.where((col >= src_bq) & (col < src_bk), ATTN_SCALE * LOG2E, 1.0).astype(F32)
    w_bf16 = (w * col_scale[None, :]).astype(BF16)
    tiles_rest = list(range(src_a_gate // D_QKV, src_merge // D_QKV))
    tiles_merge = list(range(src_merge // D_QKV, w.shape[1] // D_QKV))
    proj = _inproj(x.reshape(B * S, D), nw, w_bf16, "inproj",
                   tiles_merge + [0] + tiles_rest).reshape(B, S, D_MAIN)

    o_groups, lse_groups = [], []
    for g, (_, dilation) in enumerate(DIL_PAIRS):
        if dilation == 1:
            qkv, col0 = proj.reshape(B, 1, S, D_MAIN), COL_A0 // D_A
        else:
            x_g = x.reshape(B, S // dilation, dilation, D).transpose(0, 2, 1, 3).reshape(B * S, D)
            qkv = _inproj(x_g, nw, w_bf16, f"inproj_d{dilation}", [g])
            qkv = qkv.reshape(B, dilation, S // dilation, D_QKV)
            col0 = 0
        o_g, lse_g = _dilated_group(qkv, col0, g, B, S)
        o_groups.append(o_g)
        lse_groups.append(lse_g)
    o_b = _moba(proj, B, S)
    kv = _memkv(mem.reshape(-1, D), mem_norm_w[0][None, :], w_mem_kv[0].astype(BF16))
    kv = kv.reshape(B, mem.shape[1], 2 * D_M)
    return _merge(x, o_groups, lse_groups, o_b, proj, kv, b_merge[0][None, :],
                  w_branch_a[0].astype(BF16), w_branch_b[0].astype(BF16),
                  w_branch_m[0].astype(BF16), w_out[0].astype(BF16), final_norm_w[None, :])
```

```python
import functools
import math

import jax
import jax.numpy as jnp
from jax import lax
from jax.experimental import pallas as pl
from jax.experimental.pallas import tpu as pltpu

F32 = jnp.float32
BF16 = jnp.bfloat16

D_MODEL = 1024
HEAD_DIM = 64
ATTN_SCALE = HEAD_DIM ** -0.5
LOG2E = math.log2(math.e)
DIL_PAIRS = ((128, 1), (512, 4), (2048, 16))
N_GROUPS = len(DIL_PAIRS)
DIL_HEADS = 8
DIL_QBLK = 128
DIL_QBLOCKS = 4
MOBA_HEADS = 8
MOBA_BLOCK = 256
MOBA_TOPK = 3
MEM_HEADS = 4
N_BRANCHES = 3
EPS = 1e-6
NEG_INF = -1e30

D_A = DIL_HEADS * HEAD_DIM
D_B = MOBA_HEADS * HEAD_DIM
D_M = MEM_HEADS * HEAD_DIM
D_MERGE = N_BRANCHES * D_MODEL
D_QKV = 3 * D_A

COL_MERGE = 0
COL_A0 = D_MERGE
COL_A_GATE = COL_A0 + D_QKV
COL_BQ = COL_A_GATE + D_A
COL_BK = COL_BQ + D_B
COL_BV = COL_BK + D_B
COL_B_GATE = COL_BV + D_B
COL_MQ = COL_B_GATE + D_B
COL_M_GATE = COL_MQ + D_M
D_MAIN = COL_M_GATE + D_M

LANES = 128
HEADS_PER_LANE_TILE = LANES // HEAD_DIM

SLOPES_A = tuple(2.0 ** (-8.0 * k / 16) for k in range(1, 17, 2))
SLOPES_B = tuple(2.0 ** (-8.0 * k / 16) for k in range(2, 17, 2))

VMEM_LIMIT = 48 * 1024 * 1024


def _nt_dot(a, b):
    return lax.dot_general(a, b, (((1,), (1,)), ((), ())), preferred_element_type=F32)


def _split3(v):
    hi = v.astype(BF16)
    r1 = v - hi.astype(F32)
    mid = r1.astype(BF16)
    lo = (r1 - mid.astype(F32)).astype(BF16)
    return hi, mid, lo


def _rmsnorm_bf16(xf, w):
    ms = jnp.mean(xf * xf, axis=-1, keepdims=True)
    return (xf * lax.rsqrt(ms + EPS) * w).astype(BF16)


def _inproj_kernel(x_ref, nw_ref, w_ref, o_ref, h_ref):
    @pl.when(pl.program_id(1) == 0)
    def _():
        h_ref[...] = _rmsnorm_bf16(x_ref[...], nw_ref[...])

    o_ref[...] = jnp.dot(h_ref[...], w_ref[...], preferred_element_type=F32).astype(BF16)


def _inproj(x2, norm_w, w_bf16, name, col_tiles, tm=1024):
    T = x2.shape[0]
    tn = D_QKV
    n_cols = len(col_tiles) * tn

    def w_tile(j):
        t = col_tiles[0]
        for k, c in enumerate(col_tiles[1:], 1):
            t = jnp.where(j == k, c, t)
        return t

    return pl.pallas_call(
        _inproj_kernel,
        grid=(T // tm, len(col_tiles)),
        in_specs=[
            pl.BlockSpec((tm, D_MODEL), lambda i, j: (i, 0)),
            pl.BlockSpec((1, D_MODEL), lambda i, j: (0, 0)),
            pl.BlockSpec((D_MODEL, tn), lambda i, j: (0, w_tile(j))),
        ],
        out_specs=pl.BlockSpec((tm, tn), lambda i, j: (i, j)),
        out_shape=jax.ShapeDtypeStruct((T, n_cols), BF16),
        scratch_shapes=[pltpu.VMEM((tm, D_MODEL), BF16)],
        compiler_params=pltpu.CompilerParams(
            dimension_semantics=("parallel", "arbitrary"), vmem_limit_bytes=VMEM_LIMIT),
        name=name,
    )(x2, norm_w, w_bf16)


def _dilated_kernel(q_ref, kp_ref, kc_ref, vp_ref, vc_ref, o_ref, lse_ref, bias_ref, s_ref, p_ref,
                    *, dilation, span):
    blk = pl.program_id(2)
    Q = DIL_QBLK
    NH = HEADS_PER_LANE_TILE
    n_tiles = DIL_HEADS // NH

    @pl.when((pl.program_id(0) == 0) & (pl.program_id(1) == 0) & (blk == 0))
    def _():
        row = lax.broadcasted_iota(jnp.int32, (Q, 2 * Q), 0)
        col = lax.broadcasted_iota(jnp.int32, (Q, 2 * Q), 1)
        steps = row + Q - col
        in_band = (steps >= 0) & (steps <= span)
        dist = (steps * dilation).astype(F32)
        for h in range(DIL_HEADS):
            bias = jnp.where(in_band, -SLOPES_A[h] * dist, NEG_INF)
            bias_ref[0, h] = bias
            bias_ref[1, h] = jnp.where(col >= Q, bias, NEG_INF)

    lane = lax.broadcasted_iota(jnp.int32, (Q, LANES), 1)
    in_head = [(lane >= hh * HEAD_DIM) & (lane < (hh + 1) * HEAD_DIM) for hh in range(NH)]

    for qb in range(DIL_QBLOCKS):
        rows = slice(qb * Q, (qb + 1) * Q)
        variant = (blk == 0).astype(jnp.int32) if qb == 0 else 0

        def band(prev_ref, cur_ref, ls):
            if qb == 0:
                return jnp.concatenate([prev_ref[0, 0, :, ls], cur_ref[0, 0, :Q, ls]], axis=0)
            return cur_ref[0, 0, (qb - 1) * Q:(qb + 1) * Q, ls]

        for t in range(n_tiles):
            ls = slice(t * LANES, (t + 1) * LANES)
            q = q_ref[0, 0, rows, ls] * ATTN_SCALE
            k = band(kp_ref, kc_ref, ls)
            for hh in range(NH):
                s_ref[NH * t + hh] = _nt_dot(jnp.where(in_head[hh], q, jnp.zeros_like(q)), k)

        lse_tile = jnp.zeros((Q, LANES), F32)
        inv_den = []
        for h in range(DIL_HEADS):
            s = s_ref[h] + bias_ref[variant, h]
            m = jnp.max(s, axis=-1, keepdims=True)
            e = jnp.exp(s - m)
            den = jnp.sum(e, axis=-1, keepdims=True)
            p_ref[h] = e.astype(BF16)
            inv_den.append(1.0 / den)
            lse_tile = jnp.where(lane == h, m + jnp.log(den), lse_tile)
        lse_ref[0, rows, :] = lse_tile

        for t in range(n_tiles):
            ls = slice(t * LANES, (t + 1) * LANES)
            v = band(vp_ref, vc_ref, ls)
            o = [jnp.dot(p_ref[NH * t + hh], v, preferred_element_type=F32) * inv_den[NH * t + hh]
                 for hh in range(NH)]
            o_ref[0, rows, ls] = jnp.where(in_head[0], o[0], o[1]).astype(BF16)


def _dilated_group(qkv, col0, g, B, S):
    window, dilation = DIL_PAIRS[g]
    n = S // dilation
    nb = n // DIL_QBLK
    span = window // dilation

    rows = DIL_QBLOCKS * DIL_QBLK

    def spec(piece, prev):
        if prev:
            return pl.BlockSpec((1, 1, DIL_QBLK, D_A),
                                lambda b, r, i: (b, r, jnp.maximum(DIL_QBLOCKS * i - 1, 0), col0 + piece))
        return pl.BlockSpec((1, 1, rows, D_A), lambda b, r, i: (b, r, i, col0 + piece))

    o, lse = pl.pallas_call(
        functools.partial(_dilated_kernel, dilation=dilation, span=span),
        grid=(B, dilation, nb // DIL_QBLOCKS),
        in_specs=[spec(0, False), spec(1, True), spec(1, False), spec(2, True), spec(2, False)],
        out_specs=[
            pl.BlockSpec((1, rows, D_A), lambda b, r, i: (b, i, r)),
            pl.BlockSpec((1, rows, LANES), lambda b, r, i: (b, i, r)),
        ],
        out_shape=[
            jax.ShapeDtypeStruct((B, n, dilation * D_A), BF16),
            jax.ShapeDtypeStruct((B, n, dilation * LANES), F32),
        ],
        scratch_shapes=[
            pltpu.VMEM((2, DIL_HEADS, DIL_QBLK, 2 * DIL_QBLK), F32),
            pltpu.VMEM((DIL_HEADS, DIL_QBLK, 2 * DIL_QBLK), F32),
            pltpu.VMEM((DIL_HEADS, DIL_QBLK, 2 * DIL_QBLK), BF16),
        ],
        compiler_params=pltpu.CompilerParams(
            dimension_semantics=("arbitrary", "arbitrary", "arbitrary"), vmem_limit_bytes=VMEM_LIMIT),
        name=f"dilated_g{g}",
    )(qkv, qkv, qkv, qkv, qkv)
    return o, lse


MOBA_PAIR = 1
MOBA_QBLOCKS = 2
MOBA_STEPS_PER_BODY = 8
MOBA_VROWS = HEAD_DIM + 16
SLOPES_B_LOG2 = tuple(s * LOG2E for s in SLOPES_B)


def _moba_kernel(q_ref, k_ref, v_ref, o_ref, kmh_ref, kmm_ref, kml_ref, k2_ref, vt_ref, q2t_ref,
                 sel_ref, m_ref, acc_ref, ra_ref, rb_ref, pa_ref, pb_ref, aa_ref, ab_ref, *, nblk):
    hp = pl.program_id(1)
    tile = pl.program_id(2)
    BLK = MOBA_BLOCK
    NH = HEADS_PER_LANE_TILE
    PAIR = MOBA_PAIR
    QB = MOBA_QBLOCKS
    QT = QB * BLK
    ALIBI_LANE = (HEAD_DIM, 0)
    n0 = tile * QB
    n_past = n0 + QB - 1

    def lane_masks(rows):
        lane = lax.broadcasted_iota(jnp.int32, (rows, LANES), 1)
        in_head = [(lane >= hh * HEAD_DIM) & (lane < (hh + 1) * HEAD_DIM) for hh in range(NH)]
        alibi = [[lane == ALIBI_LANE[hh] + t for t in range(3)] for hh in range(NH)]
        return in_head, alibi

    slopes = []
    for hh in range(NH):
        slope = jnp.float32(SLOPES_B_LOG2[hh])
        for p in range(1, MOBA_HEADS // NH):
            slope = jnp.where(hp == p, jnp.float32(SLOPES_B_LOG2[NH * p + hh]), slope)
        slopes.append(slope)

    @pl.when(tile == 0)
    def _():
        in_head, alibi = lane_masks(BLK)
        key_off = lax.broadcasted_iota(jnp.int32, (BLK, LANES), 0).astype(F32)
        ones_tile = jnp.where(lax.broadcasted_iota(jnp.int32, (16, BLK), 0) == 0, 1.0, 0.0)
        cbs = []
        for hh in range(NH):
            terms = _split3(slopes[hh] * key_off)
            cb = jnp.zeros((BLK, LANES), F32)
            for t in range(3):
                cb = jnp.where(alibi[hh][t], terms[t].astype(F32), cb)
            cbs.append(cb)

        def body(j, c):
            rows = pl.ds(pl.multiple_of(j * BLK, BLK), BLK)
            kb = k_ref[0, rows, :].astype(F32)
            km = jnp.sum(kb, axis=0, keepdims=True) * (1.0 / BLK)
            hi, mid, lo = _split3(km)
            kmh_ref[pl.ds(j, 1), :] = hi.astype(F32)
            kmm_ref[pl.ds(j, 1), :] = mid.astype(F32)
            kml_ref[pl.ds(j, 1), :] = lo.astype(F32)
            vt = v_ref[0, rows, :].astype(F32).T
            for hh in range(NH):
                k2_ref[j, hh] = jnp.where(in_head[hh], kb, cbs[hh]).astype(BF16)
                vh = vt[hh * HEAD_DIM:(hh + 1) * HEAD_DIM, :]
                vt_ref[j, hh] = jnp.concatenate([vh, ones_tile], axis=0).astype(BF16)
            return c
        lax.fori_loop(0, nblk, body, 0)
        rb_ref[...] = jnp.zeros((PAIR, NH, BLK, QT), F32)
        pa_ref[...] = jnp.zeros((PAIR, NH, BLK, QT), BF16)
        aa_ref[...] = jnp.ones((PAIR, NH, 1, QT), F32)

    in_head_q, alibi_q = lane_masks(QT)
    qpos = lax.broadcasted_iota(jnp.int32, (1, NH * QT), 1)
    n_q = n0 + (qpos % QT) // BLK
    n_q_f = n_q.astype(F32)
    brow = lax.broadcasted_iota(jnp.int32, (nblk, NH * QT), 0)
    brow_f = brow.astype(F32)

    q_all = q_ref[0]
    kmh = kmh_ref[...].astype(BF16)
    kmm = kmm_ref[...].astype(BF16)
    kml = kml_ref[...].astype(BF16)

    qms = []
    for hh in range(NH):
        qm = jnp.where(in_head_q[hh], q_all, jnp.zeros_like(q_all))
        is_alibi = alibi_q[hh][0] | alibi_q[hh][1] | alibi_q[hh][2]
        q2t_ref[hh] = jnp.where(is_alibi, 1.0, qm.astype(F32)).T.astype(BF16)
        qms.append(qm)
    for hh in range(NH):
        for qb in range(QB):
            qs = slice(qb * BLK, (qb + 1) * BLK)
            ra_ref[0, hh, :, qs] = jnp.dot(k2_ref[n0 + qb, hh], q2t_ref[hh, :, qs],
                                           preferred_element_type=F32)
    gates = [_nt_dot(kmh, qm) + _nt_dot(kmm, qm) + _nt_dot(kml, qm) for qm in qms]
    g = jnp.where(brow < n_q, jnp.concatenate(gates, axis=1), NEG_INF)
    for t in range(MOBA_TOPK):
        mx = jnp.max(g, axis=0, keepdims=True)
        idx = jnp.min(jnp.where(g == mx, brow_f, float(nblk)), axis=0, keepdims=True)
        sel_ref[pl.ds(t, 1), :] = jnp.where(t < n_q, idx, -1.0)
        g = jnp.where(brow_f == idx, -jnp.inf, g)

    kk = lax.broadcasted_iota(jnp.int32, (BLK, QT), 0)
    tt = lax.broadcasted_iota(jnp.int32, (BLK, QT), 1) % BLK
    for hh in range(NH):
        r = jnp.where(kk <= tt, ra_ref[0, hh], NEG_INF)
        m = jnp.max(r, axis=0, keepdims=True)
        pb_ref[0, hh] = jnp.exp2(r - m).astype(BF16)
        m_ref[hh] = m
    for hh in range(NH):
        for qb in range(QB):
            qs = slice(qb * BLK, (qb + 1) * BLK)
            acc_ref[hh, :, qs] = jnp.dot(vt_ref[n0 + qb, hh], pb_ref[0, hh, :, qs],
                                         preferred_element_type=F32)

    last = n_past - 1

    def step(i, r_new, r_old, p_new, p_old, al_new, al_old):
        def scores():
            for k in range(PAIR):
                a = jnp.minimum(PAIR * i + k, last)
                for hh in range(NH):
                    r_new[k, hh] = jnp.dot(k2_ref[a, hh], q2t_ref[hh], preferred_element_type=F32)

        def pv():
            for hh in range(NH):
                acc = acc_ref[hh]
                for k in range(PAIR):
                    c = jnp.clip(PAIR * (i - 2) + k, 0, last)
                    acc = al_new[k, hh] * acc + jnp.dot(vt_ref[c, hh], p_new[k, hh],
                                                        preferred_element_type=F32)
                acc_ref[hh] = acc

        def softmax():
            for hh in range(NH):
                qs = slice(hh * QT, (hh + 1) * QT)
                m = m_ref[hh]
                for k in range(PAIR):
                    b = PAIR * (i - 1) + k
                    bf = jnp.where((b >= 0) & (b < n_past), b, -2).astype(F32)
                    r = r_old[k, hh]
                    chosen = ((sel_ref[0:1, qs] == bf) | (sel_ref[1:2, qs] == bf)
                              | (sel_ref[2:3, qs] == bf))
                    dist = (n_q_f[:, qs] - b.astype(F32)) * float(BLK)
                    off = jnp.where(chosen, -slopes[hh] * dist, NEG_INF)
                    m_new = jnp.maximum(m, jnp.max(r, axis=0, keepdims=True) + off)
                    al_old[k, hh] = jnp.exp2(m - m_new)
                    p_old[k, hh] = jnp.exp2(r - (m_new - off)).astype(BF16)
                    m = m_new
                m_ref[hh] = m

        for stage in (pv, softmax, scores):
            stage()

    def steps(first, count):
        for u in range(0, count, 2):
            step(first + u, ra_ref, rb_ref, pa_ref, pb_ref, aa_ref, ab_ref)
            step(first + u + 1, rb_ref, ra_ref, pb_ref, pa_ref, ab_ref, aa_ref)

    def body(t, carry):
        steps(MOBA_STEPS_PER_BODY * t, MOBA_STEPS_PER_BODY)
        return carry

    n_steps = (n_past + PAIR - 1) // PAIR + 2
    n_bodies = n_steps // MOBA_STEPS_PER_BODY
    lax.fori_loop(0, n_bodies, body, 0)
    rest = n_steps - n_bodies * MOBA_STEPS_PER_BODY
    for count in range(2, MOBA_STEPS_PER_BODY + 1, 2):
        @pl.when((rest > count - 2) & (rest <= count))
        def _(count=count):
            steps(n_bodies * MOBA_STEPS_PER_BODY, count)
    outs = []
    for hh in range(NH):
        a = acc_ref[hh]
        outs.append(a[:HEAD_DIM, :] / a[HEAD_DIM:HEAD_DIM + 1, :])
    o_ref[0] = jnp.concatenate(outs, axis=0).T.astype(BF16)


def _moba(proj, B, S):
    nblk = S // MOBA_BLOCK
    nh = HEADS_PER_LANE_TILE
    n_hp = MOBA_HEADS // nh
    qt = MOBA_QBLOCKS * MOBA_BLOCK
    cq, ck, cv = COL_BQ // LANES, COL_BK // LANES, COL_BV // LANES
    return pl.pallas_call(
        functools.partial(_moba_kernel, nblk=nblk),
        grid=(B, n_hp, S // qt),
        in_specs=[
            pl.BlockSpec((1, qt, LANES), lambda b, h, n: (b, n, cq + h)),
            pl.BlockSpec((1, S, LANES), lambda b, h, n: (b, 0, ck + h), pipeline_mode=pl.Buffered(1)),
            pl.BlockSpec((1, S, LANES), lambda b, h, n: (b, 0, cv + h), pipeline_mode=pl.Buffered(1)),
        ],
        out_specs=pl.BlockSpec((1, qt, LANES), lambda b, h, n: (b, n, h)),
        out_shape=jax.ShapeDtypeStruct((B, S, D_B), BF16),
        scratch_shapes=[pltpu.VMEM((nblk, LANES), F32)] * 3 + [
            pltpu.VMEM((nblk, nh, MOBA_BLOCK, LANES), BF16),
            pltpu.VMEM((nblk, nh, MOBA_VROWS, MOBA_BLOCK), BF16),
            pltpu.VMEM((nh, LANES, qt), BF16),
            pltpu.VMEM((8, nh * qt), F32),
            pltpu.VMEM((nh, 1, qt), F32),
            pltpu.VMEM((nh, MOBA_VROWS, qt), F32),
        ] + [pltpu.VMEM((MOBA_PAIR, nh, MOBA_BLOCK, qt), F32)] * 2
        + [pltpu.VMEM((MOBA_PAIR, nh, MOBA_BLOCK, qt), BF16)] * 2
        + [pltpu.VMEM((MOBA_PAIR, nh, 1, qt), F32)] * 2,
        compiler_params=pltpu.CompilerParams(
            dimension_semantics=("parallel", "parallel", "arbitrary"), vmem_limit_bytes=VMEM_LIMIT),
        name="moba",
    )(proj, proj, proj)


def _memkv_kernel(m_ref, nw_ref, w_ref, o_ref):
    h = _rmsnorm_bf16(m_ref[...], nw_ref[...])
    o_ref[...] = jnp.dot(h, w_ref[...], preferred_element_type=F32).astype(BF16)


def _memkv(mem2, mem_norm_w, w_bf16):
    R = mem2.shape[0]
    return pl.pallas_call(
        _memkv_kernel,
        out_shape=jax.ShapeDtypeStruct((R, 2 * D_M), BF16),
        compiler_params=pltpu.CompilerParams(vmem_limit_bytes=VMEM_LIMIT),
        name="memkv",
    )(mem2, mem_norm_w, w_bf16)


def _silu(x):
    return x * jax.nn.sigmoid(x)


def _merge_kernel(x_ref, o1_ref, o2_ref, o3_ref, l1_ref, l2_ref, l3_ref, ob_ref, ga_ref, gb_ref,
                  mq_ref, mg_ref, lg_ref, kv_ref, bm_ref, wa_ref, wb_ref, wm_ref, wo_ref, fw_ref,
                  out_ref):
    tm = x_ref.shape[1]

    perms = {}

    def perm_matrix(d):
        if d not in perms:
            t = lax.broadcasted_iota(jnp.int32, (tm, tm), 0)
            src = lax.broadcasted_iota(jnp.int32, (tm, tm), 1)
            perms[d] = jnp.where((t % d) * (tm // d) + t // d == src, 1.0, 0.0).astype(BF16)
        return perms[d]

    def natural(ref, width, exact_f32):
        d = ref.shape[2] // width
        if d == 1:
            return ref[0].astype(F32)
        stacked = jnp.concatenate([ref[0, :, r * width:(r + 1) * width] for r in range(d)], axis=0)
        perm = perm_matrix(d)
        if not exact_f32:
            return jnp.dot(perm, stacked, preferred_element_type=F32)
        hi, mid, _ = _split3(stacked)
        return (jnp.dot(perm, hi, preferred_element_type=F32)
                + jnp.dot(perm, mid, preferred_element_type=F32))

    l1, l2, l3 = (natural(r, LANES, True) for r in (l1_ref, l2_ref, l3_ref))
    lmax = jnp.maximum(jnp.maximum(l1, l2), l3)
    e1, e2, e3 = jnp.exp(l1 - lmax), jnp.exp(l2 - lmax), jnp.exp(l3 - lmax)
    inv = 1.0 / (e1 + e2 + e3)
    erow = lax.broadcasted_iota(jnp.int32, (LANES, D_A), 0)
    ecol = lax.broadcasted_iota(jnp.int32, (LANES, D_A), 1)
    expand = jnp.where((ecol >= erow * HEAD_DIM) & (ecol < (erow + 1) * HEAD_DIM), 1.0, 0.0).astype(BF16)

    def per_head(w):
        return jnp.dot(w.astype(BF16), expand, preferred_element_type=F32)

    def merge_gate(k):
        cs = slice(k * D_MODEL, (k + 1) * D_MODEL)
        return jax.nn.sigmoid(lg_ref[0, :, cs].astype(F32) + bm_ref[:, cs])

    o_a = (per_head(e1 * inv) * natural(o1_ref, D_A, False)
           + per_head(e2 * inv) * natural(o2_ref, D_A, False)
           + per_head(e3 * inv) * natural(o3_ref, D_A, False))
    t_a = (o_a * _silu(ga_ref[0].astype(F32))).astype(BF16)
    merged = merge_gate(0) * jnp.dot(t_a, wa_ref[...], preferred_element_type=F32)

    t_b = (ob_ref[0].astype(F32) * _silu(gb_ref[0].astype(F32))).astype(BF16)
    merged = merged + merge_gate(1) * jnp.dot(t_b, wb_ref[...], preferred_element_type=F32)

    mq = mq_ref[0] * ATTN_SCALE
    km = kv_ref[0, :, :D_M]
    vm = kv_ref[0, :, D_M:]
    mlane = lax.broadcasted_iota(jnp.int32, (tm, D_M), 1)
    o_m = jnp.zeros((tm, D_M), F32)
    for h in range(MEM_HEADS):
        in_head = (mlane >= h * HEAD_DIM) & (mlane < (h + 1) * HEAD_DIM)
        s = _nt_dot(jnp.where(in_head, mq, jnp.zeros_like(mq)), km)
        s = s - jnp.max(s, axis=-1, keepdims=True)
        e = jnp.exp(s)
        p = (e * (1.0 / jnp.sum(e, axis=-1, keepdims=True))).astype(BF16)
        o_m = jnp.where(in_head, jnp.dot(p, vm, preferred_element_type=F32), o_m)
    t_m = (o_m * _silu(mg_ref[0].astype(F32))).astype(BF16)
    merged = merged + merge_gate(2) * jnp.dot(t_m, wm_ref[...], preferred_element_type=F32)

    y = x_ref[0] + jnp.dot(merged.astype(BF16), wo_ref[...], preferred_element_type=F32)
    ms = jnp.mean(y * y, axis=-1, keepdims=True)
    out_ref[0] = y * lax.rsqrt(ms + EPS) * fw_ref[...]


def _merge(x, o_groups, lse_groups, o_b, proj, kv, b_merge, wa, wb, wm, wo, fw, tm=256):
    B, S, _ = x.shape

    def tok(width, colblk):
        return pl.BlockSpec((1, tm, width), lambda b, i: (b, i, colblk))

    def dil(width, d):
        return pl.BlockSpec((1, tm // d, d * width), lambda b, i: (b, i, 0))

    def full(shape):
        return pl.BlockSpec(shape, lambda b, i: (0,) * len(shape))

    in_specs = (
        [tok(D_MODEL, 0)]
        + [dil(D_A, d) for _, d in DIL_PAIRS]
        + [dil(LANES, d) for _, d in DIL_PAIRS]
        + [tok(D_B, 0)]
        + [tok(D_A, COL_A_GATE // D_A), tok(D_B, COL_B_GATE // D_B),
           tok(D_M, COL_MQ // D_M), tok(D_M, COL_M_GATE // D_M), tok(D_MERGE, COL_MERGE // D_MERGE)]
        + [pl.BlockSpec((1, kv.shape[1], 2 * D_M), lambda b, i: (b, 0, 0))]
        + [full((1, D_MERGE)), full((D_A, D_MODEL)), full((D_B, D_MODEL)), full((D_M, D_MODEL)),
           full((D_MODEL, D_MODEL)), full((1, D_MODEL))]
    )
    return pl.pallas_call(
        _merge_kernel,
        grid=(B, S // tm),
        in_specs=in_specs,
        out_specs=tok(D_MODEL, 0),
        out_shape=jax.ShapeDtypeStruct((B, S, D_MODEL), F32),
        compiler_params=pltpu.CompilerParams(
            dimension_semantics=("parallel", "parallel"), vmem_limit_bytes=VMEM_LIMIT),
        name="merge",
    )(x, *o_groups, *lse_groups, o_b, proj, proj, proj, proj, proj, kv, b_merge, wa, wb, wm, wo, fw)


def kernel(x, mem, norm_w, mem_norm_w, w_in, b_merge, w_mem_kv, w_branch_a, w_branch_b,
           w_branch_m, w_out, final_norm_w):
    B, S, D = x.shape
    assert w_in.shape[0] == 1, "single-layer block"
    w = w_in[0]
    nw = norm_w[0][None, :]
    src_a_gate = N_GROUPS * D_QKV
    src_bq = src_a_gate + D_A
    src_bk = src_bq + D_B
    src_merge = src_bk + 3 * D_B + 2 * D_M
    assert src_a_gate % D_QKV == 0 and src_merge % D_QKV == 0 and w.shape[1] % D_QKV == 0
    col = jnp.arange(w.shape[1])
    col_scale = jnp.where((col >= src_bq) & (col < src_bk), ATTN_SCALE * LOG2E, 1.0).astype(F32)
    w_bf16 = (w * col_scale[None, :]).astype(BF16)
    tiles_rest = list(range(src_a_gate // D_QKV, src_merge // D_QKV))
    tiles_merge = list(range(src_merge // D_QKV, w.shape[1] // D_QKV))
    proj = _inproj(x.reshape(B * S, D), nw, w_bf16, "inproj",
                   tiles_merge + [0] + tiles_rest).reshape(B, S, D_MAIN)

    o_groups, lse_groups = [], []
    for g, (_, dilation) in enumerate(DIL_PAIRS):
        if dilation == 1:
            qkv, col0 = proj.reshape(B, 1, S, D_MAIN), COL_A0 // D_A
        else:
            x_g = x.reshape(B, S // dilation, dilation, D).transpose(0, 2, 1, 3).reshape(B * S, D)
            qkv = _inproj(x_g, nw, w_bf16, f"inproj_d{dilation}", [g])
            qkv = qkv.reshape(B, dilation, S // dilation, D_QKV)
            col0 = 0
        o_g, lse_g = _dilated_group(qkv, col0, g, B, S)
        o_groups.append(o_g)
        lse_groups.append(lse_g)
    o_b = _moba(proj, B, S)
    kv = _memkv(mem.reshape(-1, D), mem_norm_w[0][None, :], w_mem_kv[0].astype(BF16))
    kv = kv.reshape(B, mem.shape[1], 2 * D_M)
    return _merge(x, o_groups, lse_groups, o_b, proj, kv, b_merge[0][None, :],
                  w_branch_a[0].astype(BF16), w_branch_b[0].astype(BF16),
                  w_branch_m[0].astype(BF16), w_out[0].astype(BF16), final_norm_w[None, :])
```
